```python
import jax, jax.numpy as jnp
from jax import lax
import numpy as np

D_MODEL = 1024
BATCH = 8
SEQ = 2048
DEPTH = 2

HEAD_DIM = 64
D_MIX = D_MODEL
D_A = (3 * D_MIX) // 8
D_B = (D_MIX - D_A) // 2
D_C = D_MIX - D_A - D_B
H_A = D_A // HEAD_DIM
H_B = D_B // HEAD_DIM
H_C = D_C // HEAD_DIM
DECAY_LORA = 64
AAA_LORA = 64
N_SHIFT = 3 * D_A + DECAY_LORA + AAA_LORA
N_IN = N_SHIFT + 3 * D_B + 3 * D_C + H_C + D_MIX
BLOCK = 128
NORM_EPS = 1e-6
GN_EPS = 64e-5

kernel_name = "hybrid_rwkv7_stickbreak_fox"


def rmsnorm(x, w):
    xf = x.astype(jnp.float32)
    xf = xf * lax.rsqrt(jnp.mean(xf * xf, axis=-1, keepdims=True) + NORM_EPS)
    return xf.astype(x.dtype) * w


def rwkv7_mix(p, mu, w0, w_up, a0, a_up, k_k, k_a, r_k, ln_w, ln_b):
    B, S, _ = p.shape
    dt = p.dtype
    prev = jnp.pad(p[:, :-1], ((0, 0), (1, 0), (0, 0)))
    xs = p + mu * (prev - p)
    r, k, v, wl, al = jnp.split(xs, [D_A, 2 * D_A, 3 * D_A, 3 * D_A + DECAY_LORA], axis=-1)
    w = -jax.nn.softplus(-(w0 + jnp.tanh(wl) @ w_up)) - 0.5
    decay = jnp.exp(-jnp.exp(w.astype(jnp.float32)))
    a = jax.nn.sigmoid(a0 + al @ a_up).astype(jnp.float32)
    heads = lambda t: t.astype(jnp.float32).reshape(B, S, H_A, HEAD_DIM)
    kk = heads(k * k_k)
    kk = kk / jnp.maximum(jnp.sqrt(jnp.sum(kk * kk, axis=-1, keepdims=True)), 1e-12)
    k = heads(k.astype(jnp.float32) * (1.0 + (a - 1.0) * k_a.astype(jnp.float32)))
    r, v, a, decay = heads(r), heads(v), heads(a), heads(decay)
    a_vec = -kk
    b_vec = kk * a

    def step(state, inp):
        r_t, w_t, k_t, v_t, a_t, b_t = inp
        sa = jnp.einsum('bhvk,bhk->bhv', state, a_t)
        state = (state * w_t[:, :, None, :] + sa[..., None] * b_t[:, :, None, :]
                 + v_t[..., None] * k_t[:, :, None, :])
        return state, jnp.einsum('bhvk,bhk->bhv', state, r_t)

    tm = lambda t: jnp.moveaxis(t, 1, 0)
    s0 = jnp.zeros((B, H_A, HEAD_DIM, HEAD_DIM), jnp.float32)
    _, y = lax.scan(step, s0, (tm(r), tm(decay), tm(k), tm(v), tm(a_vec), tm(b_vec)))
    y = jnp.moveaxis(y, 0, 1)
    mean = jnp.mean(y, axis=-1, keepdims=True)
    var = jnp.mean(jnp.square(y - mean), axis=-1, keepdims=True)
    y = (y - mean) * lax.rsqrt(var + GN_EPS)
    y = y * ln_w.astype(jnp.float32).reshape(H_A, HEAD_DIM) + ln_b.astype(jnp.float32).reshape(H_A, HEAD_DIM)
    bonus = jnp.sum(r * k * r_k.astype(jnp.float32), axis=-1, keepdims=True) * v
    return (y + bonus).reshape(B, S, D_A).astype(dt)


def stick_breaking_attention(q, k, v):
    S = q.shape[1]
    scale = HEAD_DIM ** -0.5
    outs = []
    for i in range(S // BLOCK):
        s0, end = i * BLOCK, (i + 1) * BLOCK
        z = jnp.einsum('bqhd,bkhd->bhqk', q[:, s0:end], k[:, :end]).astype(jnp.float32) * scale
        t_idx = s0 + jnp.arange(BLOCK)[:, None]
        s_idx = jnp.arange(end)[None, :]
        strict = s_idx < t_idx
        log1m = jnp.where(strict, jax.nn.log_sigmoid(-z), 0.0)
        after = lax.cumsum(log1m, axis=3, reverse=True) - log1m
        attn = jnp.where(strict, jnp.exp(jax.nn.log_sigmoid(z) + after), 0.0)
        outs.append(jnp.einsum('bhqk,bkhd->bqhd', attn.astype(v.dtype), v[:, :end]))
    return jnp.concatenate(outs, axis=1)


def forgetting_attention(q, k, v, log_f):
    S = q.shape[1]
    scale = HEAD_DIM ** -0.5
    c = jnp.transpose(lax.cumsum(log_f, axis=1), (0, 2, 1))
    outs = []
    for i in range(S // BLOCK):
        s0, end = i * BLOCK, (i + 1) * BLOCK
        logits = jnp.einsum('bqhd,bkhd->bhqk', q[:, s0:end], k[:, :end]).astype(jnp.float32) * scale
        logits = logits + c[:, :, s0:end, None] - c[:, :, None, :end]
        causal = jnp.arange(end)[None, :] <= (s0 + jnp.arange(BLOCK)[:, None])
        probs = jax.nn.softmax(jnp.where(causal, logits, -jnp.inf), axis=-1)
        outs.append(jnp.einsum('bhqk,bkhd->bqhd', probs.astype(v.dtype), v[:, :end]))
    return jnp.concatenate(outs, axis=1)


def setup_inputs(seed: int = 0) -> dict:
    key = jax.random.key(seed)
    ks = jax.random.split(key, 17)
    nrm = jax.random.normal
    f32 = jnp.float32
    return {
        "x": nrm(ks[0], (BATCH, SEQ, D_MODEL), f32),
        "norm_w": 1.0 + 0.02 * nrm(ks[1], (DEPTH, D_MODEL), f32),
        "w_in": nrm(ks[2], (DEPTH, D_MODEL, N_IN), f32) * D_MODEL ** -0.5,
        "b_f": jax.random.uniform(ks[3], (DEPTH, H_C), f32, 1.0, 4.0),
        "mu": jax.random.uniform(ks[4], (DEPTH, N_SHIFT), f32),
        "w0": jax.random.uniform(ks[5], (DEPTH, D_A), f32, -6.0, 1.0),
        "w_up": 0.1 * nrm(ks[6], (DEPTH, DECAY_LORA, D_A), f32),
        "a0": 0.1 * nrm(ks[7], (DEPTH, D_A), f32),
        "a_up": 0.1 * nrm(ks[8], (DEPTH, AAA_LORA, D_A), f32),
        "k_k": 0.85 + 0.05 * nrm(ks[9], (DEPTH, D_A), f32),
        "k_a": 1.0 + 0.05 * nrm(ks[10], (DEPTH, D_A), f32),
        "r_k": 0.1 * nrm(ks[11], (DEPTH, H_A, HEAD_DIM), f32),
        "ln_x_w": 1.0 + 0.02 * nrm(ks[12], (DEPTH, D_A), f32),
        "ln_x_b": 0.02 * nrm(ks[13], (DEPTH, D_A), f32),
        "w_out": nrm(ks[14], (DEPTH, D_MIX, D_MODEL), f32) * D_MIX ** -0.5,
        "final_norm_w": 1.0 + 0.02 * nrm(ks[15], (D_MODEL,), f32),
    }


def reference(x, norm_w, w_in, b_f, mu, w0, w_up, a0, a_up, k_k, k_a, r_k,
              ln_x_w, ln_x_b, w_out, final_norm_w):
    B, S, _ = x.shape
    splits = [N_SHIFT,
              N_SHIFT + D_B, N_SHIFT + 2 * D_B, N_SHIFT + 3 * D_B,
              N_SHIFT + 3 * D_B + D_C, N_SHIFT + 3 * D_B + 2 * D_C, N_SHIFT + 3 * D_B + 3 * D_C,
              N_SHIFT + 3 * D_B + 3 * D_C + H_C]
    for l in range(DEPTH):
        h = rmsnorm(x, norm_w[l])
        p = h @ w_in[l]
        p_a, q_b, k_b, v_b, q_c, k_c, v_c, f_c, gate = jnp.split(p, splits, axis=-1)
        y_a = rwkv7_mix(p_a, mu[l], w0[l], w_up[l], a0[l], a_up[l], k_k[l], k_a[l],
                        r_k[l], ln_x_w[l], ln_x_b[l])
        hb = lambda t, n: t.reshape(B, S, n, HEAD_DIM)
        y_b = stick_breaking_attention(hb(q_b, H_B), hb(k_b, H_B), hb(v_b, H_B)).reshape(B, S, D_B)
        log_f = jax.nn.log_sigmoid((f_c + b_f[l]).astype(jnp.float32))
        y_c = forgetting_attention(hb(q_c, H_C), hb(k_c, H_C), hb(v_c, H_C), log_f).reshape(B, S, D_C)
        y = jnp.concatenate([y_a, y_b, y_c], axis=-1) * jax.nn.silu(gate)
        x = x + y @ w_out[l]
    return rmsnorm(x, final_norm_w)
```

```python
import functools

import jax
import jax.numpy as jnp
from jax import lax
from jax.experimental import pallas as pl
from jax.experimental.pallas import tpu as pltpu

HEAD_DIM = 64
LANES = 128
NORM_EPS = 1e-6
GN_EPS = 64e-5
VMEM_LIMIT = 56 * 1024 * 1024

F32 = jnp.float32
BF16 = jnp.bfloat16


def _iota(shape, dim):
    return lax.broadcasted_iota(jnp.int32, shape, dim)


def _blk(idx, size):
    assert size & (size - 1) == 0
    return jnp.right_shift(idx, size.bit_length() - 1)


def _rem(idx, size):
    assert size & (size - 1) == 0
    return jnp.bitwise_and(idx, size - 1)


def _dot(a, b):
    return jnp.dot(a.astype(BF16), b.astype(BF16), preferred_element_type=F32)


def _dot_nt(a, b):
    return lax.dot_general(a.astype(BF16), b.astype(BF16), (((1,), (1,)), ((), ())),
                           preferred_element_type=F32)


def _split(x, n):
    parts = []
    rem = x
    for i in range(n):
        p = rem.astype(BF16)
        parts.append(p)
        if i + 1 < n:
            rem = rem - p.astype(F32)
    return parts


def _dot_xr(x, r, n):
    out = None
    for p in _split(x, n):
        t = jnp.dot(p, r, preferred_element_type=F32)
        out = t if out is None else out + t
    return out


def _dot_lx(l, x, n):
    out = None
    for p in _split(x, n):
        t = jnp.dot(l, p, preferred_element_type=F32)
        out = t if out is None else out + t
    return out


def _log_sigmoid(z):
    return jnp.minimum(z, 0.0) - jnp.log1p(jnp.exp(-jnp.abs(z)))


def _sigmoid(z):
    return 1.0 / (1.0 + jnp.exp(-z))


def _cparams(sem):
    return pltpu.CompilerParams(dimension_semantics=sem, vmem_limit_bytes=VMEM_LIMIT)


def _inproj_kernel(x_ref, nw_ref, w_ref, pa_ref, q_ref, k_ref, v_ref, f_ref, g_ref, *, n_chunk):
    x = x_ref[...]
    xn = x * lax.rsqrt(jnp.mean(x * x, axis=-1, keepdims=True) + NORM_EPS)
    h = (xn * nw_ref[...]).astype(BF16)
    off = 0
    for ref in (pa_ref, q_ref, k_ref, v_ref, f_ref, g_ref):
        width = ref.shape[-1]
        for c0 in range(0, width, n_chunk):
            c1 = min(c0 + n_chunk, width)
            ref[:, c0:c1] = jnp.dot(h, w_ref[:, off + c0:off + c1],
                                    preferred_element_type=F32).astype(ref.dtype)
        off += width


def _inproj(x2, nw, w, widths, dtypes, tm=512, n_chunk=640):
    m, d = x2.shape
    n = w.shape[1]
    assert sum(widths) == n and m % tm == 0
    out_shape = [jax.ShapeDtypeStruct((m, wd), dt) for wd, dt in zip(widths, dtypes)]
    out_specs = [pl.BlockSpec((tm, wd), lambda i: (i, 0)) for wd in widths]
    return pl.pallas_call(
        functools.partial(_inproj_kernel, n_chunk=n_chunk),
        grid=(m // tm,),
        in_specs=[pl.BlockSpec((tm, d), lambda i: (i, 0)),
                  pl.BlockSpec((1, d), lambda i: (0, 0)),
                  pl.BlockSpec((d, n), lambda i: (0, 0))],
        out_specs=out_specs,
        out_shape=out_shape,
        compiler_params=_cparams(("parallel",)),
        name="inproj",
    )(x2, nw, w)


def _outproj_kernel(x_ref, ya_ref, yb_ref, yc_ref, g_ref, w_ref, fw_ref, o_ref, *, final):
    da = ya_ref.shape[-1]
    db = yb_ref.shape[-1]
    acc = x_ref[...]
    off = 0
    for y_ref, wd in ((ya_ref, da), (yb_ref, db), (yc_ref, db)):
        g = g_ref[:, off:off + wd]
        yg = y_ref[...] * (g * _sigmoid(g))
        acc = acc + jnp.dot(yg.astype(BF16), w_ref[off:off + wd, :], preferred_element_type=F32)
        off += wd
    if final:
        acc = acc * lax.rsqrt(jnp.mean(acc * acc, axis=-1, keepdims=True) + NORM_EPS) * fw_ref[...]
    o_ref[...] = acc


def _outproj(x2, ya, yb, yc, g, w, fw, final, tm=512):
    m, d = x2.shape
    row = lambda wd: pl.BlockSpec((tm, wd), lambda i: (i, 0))
    return pl.pallas_call(
        functools.partial(_outproj_kernel, final=final),
        grid=(m // tm,),
        in_specs=[row(d), row(ya.shape[1]), row(yb.shape[1]), row(yc.shape[1]), row(g.shape[1]),
                  pl.BlockSpec(w.shape, lambda i: (0, 0)),
                  pl.BlockSpec((1, d), lambda i: (0, 0))],
        out_specs=row(d),
        out_shape=jax.ShapeDtypeStruct((m, d), F32),
        compiler_params=_cparams(("parallel",)),
        name="outproj",
    )(x2, ya, yb, yc, g, w, fw)


def _fcum_kernel(f_ref, bf_ref, ccol_ref, crow_ref, *, blk):
    s = f_ref.shape[1]
    row = _iota((blk, blk), 0)
    col = _iota((blk, blk), 1)
    tri = (col <= row).astype(BF16)
    carry = jnp.zeros((1, LANES), F32)
    for i in range(s // blk):
        lf = _log_sigmoid(f_ref[0, i * blk:(i + 1) * blk, :] + bf_ref[...])
        c = _dot_lx(tri, lf, 3) + carry
        ccol_ref[0, i * blk:(i + 1) * blk, :] = c
        carry = c[blk - 1:blk, :]
    ct = ccol_ref[0].T
    crow_ref[0] = ct[0:8, :]


def _fcum(f, bf, blk=256):
    b, s, _ = f.shape
    return pl.pallas_call(
        functools.partial(_fcum_kernel, blk=blk),
        grid=(b,),
        in_specs=[pl.BlockSpec((1, s, LANES), lambda i: (i, 0, 0)),
                  pl.BlockSpec((1, LANES), lambda i: (0, 0))],
        out_specs=[pl.BlockSpec((1, s, LANES), lambda i: (i, 0, 0)),
                   pl.BlockSpec((1, 8, s), lambda i: (i, 0, 0))],
        out_shape=[jax.ShapeDtypeStruct((b, s, LANES), F32),
                   jax.ShapeDtypeStruct((b, 8, s), F32)],
        compiler_params=_cparams(("parallel",)),
        name="fcum",
    )(f, bf)


def _sb_kernel(q_ref, k_ref, v_ref, o_ref, acc_ref, *, t, n_single):
    pi = pl.program_id(1)
    qi = pl.program_id(2)
    lane = _iota((t, LANES), 1)
    row = _iota((t, t), 0)
    col = _iota((t, t), 1)
    strict = col < row
    tri = (row > col).astype(BF16)
    q2 = q_ref[0]

    def head(h):
        qh = jnp.where((lane >= h * HEAD_DIM) & (lane < (h + 1) * HEAD_DIM), q2, jnp.zeros_like(q2))
        q0 = pl.multiple_of(qi * t, t)
        k2 = k_ref[0, pl.ds(q0, t), :]
        v2 = v_ref[0, pl.ds(q0, t), :]
        z = _dot_nt(qh, k2)
        ls = _log_sigmoid(z)
        log1m = jnp.where(strict, ls - z, 0.0)
        after = _dot_xr(log1m, tri, 1)
        attn = jnp.where(strict, jnp.exp(ls + after), 0.0)
        acc = _dot(attn, v2)
        carry = jnp.sum(log1m, axis=-1, keepdims=True)

        def body(j, st):
            acc, carry = st
            kb = qi - 1 - j
            k0 = pl.multiple_of(kb * t, t)
            k2 = k_ref[0, pl.ds(k0, t), :]
            v2 = v_ref[0, pl.ds(k0, t), :]
            z = _dot_nt(qh, k2)
            ls = _log_sigmoid(z)
            log1m = ls - z
            after = _dot_xr(log1m, tri, 1) + carry
            attn = jnp.exp(ls + after)
            acc = acc + _dot(attn, v2)
            carry = carry + jnp.sum(log1m, axis=-1, keepdims=True)
            return acc, carry

        acc, _ = lax.fori_loop(0, qi, body, (acc, carry))
        return acc

    acc_ref[...] = jnp.zeros_like(acc_ref)
    @pl.when(pi < n_single)
    def _():
        acc_ref[...] = head(1)

    out0 = head(0)
    o_ref[0] = jnp.where(lane < HEAD_DIM, out0, acc_ref[...])


def _sb_attention(q, k, v, n_pairs, t=256):
    b, s, _ = q.shape
    return pl.pallas_call(
        functools.partial(_sb_kernel, t=t, n_single=n_pairs - 1),
        grid=(b, n_pairs, s // t),
        in_specs=[pl.BlockSpec((1, t, LANES), lambda bi, pi, qi: (bi, qi, pi)),
                  pl.BlockSpec((1, s, LANES), lambda bi, pi, qi: (bi, 0, pi)),
                  pl.BlockSpec((1, s, LANES), lambda bi, pi, qi: (bi, 0, pi))],
        out_specs=pl.BlockSpec((1, t, LANES), lambda bi, pi, qi: (bi, qi, pi)),
        out_shape=jax.ShapeDtypeStruct((b, s, n_pairs * LANES), F32),
        scratch_shapes=[pltpu.VMEM((t, LANES), F32)],
        compiler_params=_cparams(("parallel", "parallel", "arbitrary")),
        name="stickbreak",
    )(q, k, v)


def _fox_kernel(q_ref, k_ref, v_ref, ccol_ref, crow_ref, o_ref, acc_ref, *, t):
    pi = pl.program_id(1)
    qi = pl.program_id(2)
    lane = _iota((t, LANES), 1)
    row = _iota((t, t), 0)
    col = _iota((t, t), 1)
    causal = col <= row
    q2 = q_ref[0]
    q0 = pl.multiple_of(qi * t, t)

    def head(h):
        ci = 2 * pi + h
        qh = jnp.where((lane >= h * HEAD_DIM) & (lane < (h + 1) * HEAD_DIM), q2, jnp.zeros_like(q2))
        cq = jnp.sum(jnp.where(lane == ci, ccol_ref[0], 0.0), axis=-1, keepdims=True)

        def scores(k0):
            k2 = k_ref[0, pl.ds(k0, t), :]
            ck = crow_ref[0, pl.ds(ci, 1), pl.ds(k0, t)]
            return _dot_nt(qh, k2) + cq - ck

        def update(st, s_blk, k0):
            m, l, acc = st
            v2 = v_ref[0, pl.ds(k0, t), :]
            m_new = jnp.maximum(m, jnp.max(s_blk, axis=-1, keepdims=True))
            alpha = jnp.exp(m - m_new)
            p = jnp.exp(s_blk - m_new)
            l = alpha * l + jnp.sum(p, axis=-1, keepdims=True)
            acc = alpha * acc + _dot(p, v2)
            return m_new, l, acc

        def body(j, st):
            k0 = pl.multiple_of(j * t, t)
            return update(st, scores(k0), k0)

        st = (jnp.full((t, 1), -1e30, F32), jnp.zeros((t, 1), F32), jnp.zeros((t, LANES), F32))
        st = lax.fori_loop(0, qi, body, st)
        s_diag = jnp.where(causal, scores(q0), -1e30)
        m, l, acc = update(st, s_diag, q0)
        return acc / l

    acc_ref[...] = jnp.zeros_like(acc_ref)
    @pl.when(pi > 0)
    def _():
        acc_ref[...] = head(0)

    out1 = head(1)
    o_ref[0] = jnp.where(lane < HEAD_DIM, acc_ref[...], out1)


def _fox_attention(q, k, v, ccol, crow, first_pair, n_pairs, t=256):
    b, s, _ = q.shape
    return pl.pallas_call(
        functools.partial(_fox_kernel, t=t),
        grid=(b, n_pairs, s // t),
        in_specs=[pl.BlockSpec((1, t, LANES), lambda bi, pi, qi: (bi, qi, pi + first_pair)),
                  pl.BlockSpec((1, s, LANES), lambda bi, pi, qi: (bi, 0, pi + first_pair)),
                  pl.BlockSpec((1, s, LANES), lambda bi, pi, qi: (bi, 0, pi + first_pair)),
                  pl.BlockSpec((1, t, LANES), lambda bi, pi, qi: (bi, qi, 0)),
                  pl.BlockSpec((1, 8, s), lambda bi, pi, qi: (bi, 0, 0))],
        out_specs=pl.BlockSpec((1, t, LANES), lambda bi, pi, qi: (bi, qi, pi)),
        out_shape=jax.ShapeDtypeStruct((b, s, n_pairs * LANES), F32),
        scratch_shapes=[pltpu.VMEM((t, LANES), F32)],
        compiler_params=_cparams(("parallel", "parallel", "arbitrary")),
        name="fox",
    )(q, k, v, ccol, crow)


def _unit_lower_inverse(lbd, c):
    n = lbd.shape[0]
    row = _iota((n, n), 0)
    col = _iota((n, n), 1)
    eye = (row == col).astype(F32)
    x = eye + jnp.where(_blk(row, 2) == _blk(col, 2), lbd, 0.0)
    s = 2
    while s < c:
        o = jnp.where((_blk(row, 2 * s) == _blk(col, 2 * s)) & (_blk(row, s) != _blk(col, s)),
                      lbd, 0.0)
        x = x + _dot(x, _dot(o, x))
        s *= 2
    return x


def _rwkv_kernel(pa_ref, mu_ref, wl_ref, w0_ref, a0_ref, kk_ref, ka_ref, rk_ref, lnw_ref, lnb_ref,
                 o_ref, prev_ref, h_ref, *, tb, c, da):
    ti = pl.program_id(1)
    n_pairs = da // LANES
    width = pa_ref.shape[-1]

    @pl.when(ti == 0)
    def _():
        prev_ref[...] = jnp.zeros_like(prev_ref)
        h_ref[...] = jnp.zeros_like(h_ref)

    p = pa_ref[0]
    prow = _iota((tb, width), 0)
    prev = jnp.where(prow == 0, prev_ref[...], pltpu.roll(p, 1, 0))
    prev_ref[...] = p[tb - 1:tb, :]
    xs = p + mu_ref[...] * (prev - p)
    r = xs[:, 0:da]
    k = xs[:, da:2 * da]
    v = xs[:, 2 * da:3 * da]
    wa = xs[:, 3 * da:3 * da + LANES]
    lane_t = _iota((tb, LANES), 1)
    wa = jnp.where(lane_t < HEAD_DIM, jnp.tanh(wa), wa)
    lora = _dot(wa, wl_ref[...])
    w = -(jnp.maximum(-(w0_ref[...] + lora[:, 0:da]), 0.0)
          + jnp.log1p(jnp.exp(-jnp.abs(w0_ref[...] + lora[:, 0:da])))) - 0.5
    lw = -jnp.exp(w)
    a = _sigmoid(a0_ref[...] + lora[:, da:2 * da])
    k2 = k * (1.0 + (a - 1.0) * ka_ref[...])
    kk = k * kk_ref[...]

    er = _blk(_iota((LANES, LANES), 0), HEAD_DIM)
    ec = _blk(_iota((LANES, LANES), 1), HEAD_DIM)
    ones_head = (er == ec).astype(BF16)
    bd_mask = er == ec

    rowc = _iota((c, LANES), 0)
    colc = _iota((c, LANES), 1)
    incl_c = _rem(colc, c) <= rowc
    rr = _iota((2 * c, 2 * c), 0)
    cc = _iota((2 * c, 2 * c), 1)
    same_blk = _blk(rr, c) == _blk(cc, c)
    strict_2c = _rem(cc, c) < _rem(rr, c)
    tri_c = (_iota((c, c), 1) <= _iota((c, c), 0)).astype(BF16)
    lane_c = _iota((c, LANES), 1)
    m0 = lane_c < HEAD_DIM
    diag_eye = (_iota((LANES, LANES), 0) == _iota((LANES, LANES), 1)).astype(F32)

    for pr in range(n_pairs):
        sl = slice(pr * LANES, (pr + 1) * LANES)
        kkp = kk[:, sl]
        ss = _dot_xr(kkp * kkp, ones_head, 2)
        kkp = kkp * lax.rsqrt(jnp.maximum(ss, 1e-24))
        ap = a[:, sl]
        r_p, k_p, v_p, lw_p = r[:, sl], k2[:, sl], v[:, sl], lw[:, sl]
        av_p = -kkp
        bv_p = kkp * ap
        hbd = h_ref[pr]
        ys = []
        for j in range(tb // c):
            rs = slice(j * c, (j + 1) * c)
            rc, kc, vc, lwc, ac, bc = r_p[rs], k_p[rs], v_p[rs], lw_p[rs], av_p[rs], bv_p[rs]
            cum = _dot_lx(tri_c, lwc, 3)
            eg = jnp.exp(cum)
            egp = jnp.exp(cum - lwc)
            ieg = jnp.exp(-cum)
            gam = eg[c - 1:c, :]
            rt = rc * eg
            at = ac * egp
            kt = kc * ieg
            bt = bc * ieg
            bk = jnp.concatenate([bt, kt], axis=0)
            kb = jnp.concatenate([kt, bt], axis=0)
            at0 = jnp.where(m0, at, 0.0)
            at1 = jnp.where(m0, 0.0, at)
            pa0 = _dot_nt(at0, bk)
            pa1 = _dot_nt(at1, kb)
            xall = jnp.concatenate([pa0, pa1], axis=0)
            lbd = jnp.where(same_blk & strict_2c, xall, 0.0)
            lak = jnp.where((~same_blk) & strict_2c, xall, 0.0)
            tinv = _unit_lower_inverse(lbd, c)
            v0 = jnp.where(m0, vc, 0.0)
            v1 = jnp.where(m0, 0.0, vc)
            w1 = _dot(lak, jnp.concatenate([v1, v0], axis=0))
            qa2 = jnp.concatenate([at0, at1], axis=0)
            tz = _dot(tinv, jnp.concatenate([qa2, w1], axis=1))
            a2 = tz[0:c, 0:LANES] + tz[c:2 * c, 0:LANES]
            v2 = tz[0:c, LANES:] + tz[c:2 * c, LANES:]
            pr0 = jnp.where(incl_c, _dot_nt(jnp.where(m0, rt, 0.0), bk), 0.0)
            pr1 = jnp.where(incl_c, _dot_nt(jnp.where(m0, 0.0, rt), bk), 0.0)
            zeros_c = jnp.zeros((c, LANES), F32)
            rhs0 = jnp.concatenate(
                [jnp.concatenate([jnp.where(m0, a2, 0.0), jnp.where(m0, v2, 0.0)], axis=1),
                 jnp.concatenate([zeros_c, v0], axis=1)], axis=0)
            rhs1 = jnp.concatenate(
                [jnp.concatenate([jnp.where(m0, 0.0, a2), jnp.where(m0, 0.0, v2)], axis=1),
                 jnp.concatenate([zeros_c, v1], axis=1)], axis=0)
            ry = _dot(pr0, rhs0) + _dot(pr1, rhs1)
            r3 = rt + ry[:, 0:LANES]
            y2 = ry[:, LANES:]
            bkbar = jnp.concatenate([bt * gam, kt * gam], axis=0)
            rhs_s = jnp.concatenate(
                [jnp.concatenate([a2, v2], axis=1),
                 jnp.concatenate([zeros_c, vc], axis=1)], axis=0)
            pg = _dot(bkbar.T, rhs_s)
            phi = diag_eye * gam + jnp.where(bd_mask, pg[:, 0:LANES], 0.0)
            g = jnp.where(bd_mask, pg[:, LANES:], 0.0)
            ys.append(_dot(r3, hbd) + y2)
            hbd = _dot(phi, hbd) + g
        h_ref[pr] = hbd
        y = jnp.concatenate(ys, axis=0)
        mean = _dot_xr(y, ones_head, 3) * (1.0 / HEAD_DIM)
        yc_ = y - mean
        var = _dot_xr(yc_ * yc_, ones_head, 3) * (1.0 / HEAD_DIM)
        yn = yc_ * lax.rsqrt(var + GN_EPS) * lnw_ref[:, sl] + lnb_ref[:, sl]
        bonus = _dot_xr(r_p * k_p * rk_ref[:, sl], ones_head, 3) * v_p
        o_ref[0, :, sl] = yn + bonus


def _rwkv(pa, mu, wl, w0, a0, k_k, k_a, r_k, ln_w, ln_b, tb=256, c=64):
    b, s, width = pa.shape
    da = w0.shape[-1]
    vec = lambda n: pl.BlockSpec((1, n), lambda bi, ti: (0, 0))
    return pl.pallas_call(
        functools.partial(_rwkv_kernel, tb=tb, c=c, da=da),
        grid=(b, s // tb),
        in_specs=[pl.BlockSpec((1, tb, width), lambda bi, ti: (bi, ti, 0)),
                  vec(width),
                  pl.BlockSpec(wl.shape, lambda bi, ti: (0, 0)),
                  vec(da), vec(da), vec(da), vec(da), vec(da), vec(da), vec(da)],
        out_specs=pl.BlockSpec((1, tb, da), lambda bi, ti: (bi, ti, 0)),
        out_shape=jax.ShapeDtypeStruct((b, s, da), F32),
        scratch_shapes=[pltpu.VMEM((1, width), F32),
                        pltpu.VMEM((da // LANES, LANES, LANES), F32)],
        compiler_params=_cparams(("parallel", "arbitrary")),
        name="rwkv7",
    )(pa, mu, wl, w0, a0, k_k, k_a, r_k, ln_w, ln_b)


def kernel(x, norm_w, w_in, b_f, mu, w0, w_up, a0, a_up, k_k, k_a, r_k, ln_x_w, ln_x_b, w_out,
           final_norm_w):
    b, s, d = x.shape
    depth = w_in.shape[0]
    da = w0.shape[-1]
    lora = w_up.shape[1]
    hc = b_f.shape[-1]
    d_mix = w_out.shape[1]
    db = (d_mix - da) // 2
    n_shift = 3 * da + 2 * lora
    assert da % LANES == 0 and 2 * lora == LANES and db == hc * HEAD_DIM and hc % 2 == 1
    assert mu.shape[-1] == n_shift
    n_pairs_attn = (db + HEAD_DIM) // LANES
    dbp = n_pairs_attn * LANES
    scale = HEAD_DIM ** -0.5

    x2 = x.reshape(b * s, d)
    zpad = lambda rows, n: jnp.zeros((rows, n), F32)
    for l in range(depth):
        wl_in = w_in[l]
        o = n_shift
        qb, kb, vb = (wl_in[:, o + i * db:o + (i + 1) * db] for i in range(3))
        o += 3 * db
        qc, kc, vc = (wl_in[:, o + i * db:o + (i + 1) * db] for i in range(3))
        o += 3 * db
        wf = wl_in[:, o:o + hc]
        wg = wl_in[:, o + hc:]
        w_all = jnp.concatenate(
            [wl_in[:, :n_shift],
             qb * scale, qc * scale, kb, kc, vb, vc,
             zpad(d, 1), wf, zpad(d, LANES - 1 - hc),
             wg[:, :da + db], zpad(d, 2 * HEAD_DIM), wg[:, da + db:]], axis=1).astype(BF16)
        widths = (n_shift, 2 * db, 2 * db, 2 * db, LANES, da + 2 * dbp)
        dtypes = (F32, BF16, BF16, BF16, F32, F32)
        pa, q, k, v, f, g = _inproj(x2, norm_w[l][None, :], w_all, widths, dtypes)

        wl_comb = jnp.concatenate(
            [jnp.concatenate([w_up[l], zpad(lora, da)], axis=1),
             jnp.concatenate([zpad(lora, da), a_up[l]], axis=1)], axis=0).astype(BF16)
        row = lambda t: t.reshape(1, -1)
        ya = _rwkv(pa.reshape(b, s, n_shift), row(mu[l]), wl_comb, row(w0[l]), row(a0[l]),
                   row(k_k[l]), row(k_a[l]), row(r_k[l]), row(ln_x_w[l]), row(ln_x_b[l]))

        q3, k3, v3 = (t.reshape(b, s, 2 * db) for t in (q, k, v))
        yb = _sb_attention(q3, k3, v3, n_pairs_attn)
        bf_pad = jnp.concatenate([jnp.zeros((1,), F32), b_f[l], jnp.zeros((LANES - 1 - hc,), F32)])
        ccol, crow = _fcum(f.reshape(b, s, LANES), bf_pad[None, :])
        yc = _fox_attention(q3, k3, v3, ccol, crow, first_pair=n_pairs_attn - 1, n_pairs=n_pairs_attn)

        wo = w_out[l]
        wo_p = jnp.concatenate([wo[:da + db], zpad(2 * HEAD_DIM, d), wo[da + db:]], axis=0).astype(BF16)
        x2 = _outproj(x2, ya.reshape(b * s, da), yb.reshape(b * s, dbp), yc.reshape(b * s, dbp),
                      g, wo_p, final_norm_w[None, :], final=(l == depth - 1))
    return x2.reshape(b, s, d)
```

```python
import functools

import jax
import jax.numpy as jnp
from jax import lax
from jax.experimental import pallas as pl
from jax.experimental.pallas import tpu as pltpu

HEAD_DIM = 64
LANES = 128
NORM_EPS = 1e-6
GN_EPS = 64e-5
VMEM_LIMIT = 56 * 1024 * 1024
EXP_ZERO = -104.0

F32 = jnp.float32
BF16 = jnp.bfloat16


def _iota(shape, dim):
    return lax.broadcasted_iota(jnp.int32, shape, dim)


def _blk(idx, size):
    assert size & (size - 1) == 0
    return jnp.right_shift(idx, size.bit_length() - 1)


def _rem(idx, size):
    assert size & (size - 1) == 0
    return jnp.bitwise_and(idx, size - 1)


def _dot(a, b):
    return jnp.dot(a.astype(BF16), b.astype(BF16), preferred_element_type=F32)


def _dot_nt(a, b):
    return lax.dot_general(a.astype(BF16), b.astype(BF16), (((1,), (1,)), ((), ())),
                           preferred_element_type=F32)


def _split(x, n):
    parts = []
    rem = x
    for i in range(n):
        p = rem.astype(BF16)
        parts.append(p)
        if i + 1 < n:
            rem = rem - p.astype(F32)
    return parts


def _dot_xr(x, r, n):
    out = None
    for p in _split(x, n):
        t = jnp.dot(p, r, preferred_element_type=F32)
        out = t if out is None else out + t
    return out


def _dot_lx(l, x, n):
    out = None
    for p in _split(x, n):
        t = jnp.dot(l, p, preferred_element_type=F32)
        out = t if out is None else out + t
    return out


def _log_sigmoid(z):
    return jnp.minimum(z, 0.0) - jnp.log1p(jnp.exp(-jnp.abs(z)))


def _sigmoid(z):
    return 1.0 / (1.0 + jnp.exp(-z))


def _head_lanes(lane, h):
    return (lane >= h * HEAD_DIM) & (lane < (h + 1) * HEAD_DIM)


def _cparams(sem):
    return pltpu.CompilerParams(dimension_semantics=sem, vmem_limit_bytes=VMEM_LIMIT)


def _inproj_kernel(x_ref, nw_ref, w_ref, pa_ref, q_ref, k_ref, v_ref, f_ref, g_ref, *, n_chunk):
    x = x_ref[...]
    xn = x * lax.rsqrt(jnp.mean(x * x, axis=-1, keepdims=True) + NORM_EPS)
    h = (xn * nw_ref[...]).astype(BF16)
    off = 0
    for ref in (pa_ref, q_ref, k_ref, v_ref, f_ref, g_ref):
        width = ref.shape[-1]
        for c0 in range(0, width, n_chunk):
            c1 = min(c0 + n_chunk, width)
            ref[:, c0:c1] = jnp.dot(h, w_ref[:, off + c0:off + c1],
                                    preferred_element_type=F32).astype(ref.dtype)
        off += width


def _inproj(x2, nw, w, widths, dtypes, tm=512, n_chunk=640):
    m, d = x2.shape
    n = w.shape[1]
    assert sum(widths) == n and m % tm == 0
    out_shape = [jax.ShapeDtypeStruct((m, wd), dt) for wd, dt in zip(widths, dtypes)]
    out_specs = [pl.BlockSpec((tm, wd), lambda i: (i, 0)) for wd in widths]
    return pl.pallas_call(
        functools.partial(_inproj_kernel, n_chunk=n_chunk),
        grid=(m // tm,),
        in_specs=[pl.BlockSpec((tm, d), lambda i: (i, 0)),
                  pl.BlockSpec((1, d), lambda i: (0, 0)),
                  pl.BlockSpec((d, n), lambda i: (0, 0))],
        out_specs=out_specs,
        out_shape=out_shape,
        compiler_params=_cparams(("parallel",)),
        name="inproj",
    )(x2, nw, w)


def _outproj_kernel(x_ref, ya_ref, yb_ref, yc_ref, g_ref, w_ref, fw_ref, o_ref, *, final):
    da = ya_ref.shape[-1]
    db = yb_ref.shape[-1]
    acc = x_ref[...]
    off = 0
    for y_ref, wd in ((ya_ref, da), (yb_ref, db), (yc_ref, db)):
        g = g_ref[:, off:off + wd]
        yg = y_ref[...] * (g * _sigmoid(g))
        acc = acc + jnp.dot(yg.astype(BF16), w_ref[off:off + wd, :], preferred_element_type=F32)
        off += wd
    if final:
        acc = acc * lax.rsqrt(jnp.mean(acc * acc, axis=-1, keepdims=True) + NORM_EPS) * fw_ref[...]
    o_ref[...] = acc


def _outproj(x2, ya, yb, yc, g, w, fw, final, tm=512):
    m, d = x2.shape
    row = lambda wd: pl.BlockSpec((tm, wd), lambda i: (i, 0))
    return pl.pallas_call(
        functools.partial(_outproj_kernel, final=final),
        grid=(m // tm,),
        in_specs=[row(d), row(ya.shape[1]), row(yb.shape[1]), row(yc.shape[1]), row(g.shape[1]),
                  pl.BlockSpec(w.shape, lambda i: (0, 0)),
                  pl.BlockSpec((1, d), lambda i: (0, 0))],
        out_specs=row(d),
        out_shape=jax.ShapeDtypeStruct((m, d), F32),
        compiler_params=_cparams(("parallel",)),
        name="outproj",
    )(x2, ya, yb, yc, g, w, fw)


def _fcum_kernel(f_ref, bf_ref, ccol_ref, crow_ref, *, blk):
    s = f_ref.shape[1]
    row = _iota((blk, blk), 0)
    col = _iota((blk, blk), 1)
    tri = (col <= row).astype(BF16)
    carry = jnp.zeros((1, LANES), F32)
    for i in range(s // blk):
        lf = _log_sigmoid(f_ref[0, i * blk:(i + 1) * blk, :] + bf_ref[...])
        c = _dot_lx(tri, lf, 3) + carry
        ccol_ref[0, i * blk:(i + 1) * blk, :] = c
        carry = c[blk - 1:blk, :]
    ct = ccol_ref[0].T
    crow_ref[0] = ct[0:8, :]


def _fcum(f, bf, blk=256):
    b, s, _ = f.shape
    return pl.pallas_call(
        functools.partial(_fcum_kernel, blk=blk),
        grid=(b,),
        in_specs=[pl.BlockSpec((1, s, LANES), lambda i: (i, 0, 0)),
                  pl.BlockSpec((1, LANES), lambda i: (0, 0))],
        out_specs=[pl.BlockSpec((1, s, LANES), lambda i: (i, 0, 0)),
                   pl.BlockSpec((1, 8, s), lambda i: (i, 0, 0))],
        out_shape=[jax.ShapeDtypeStruct((b, s, LANES), F32),
                   jax.ShapeDtypeStruct((b, 8, s), F32)],
        compiler_params=_cparams(("parallel",)),
        name="fcum",
    )(f, bf)


def _sb_kernel(q_ref, k_ref, v_ref, o_ref, *, t, n_single):
    pi = pl.program_id(1)
    qi = pl.program_id(2)
    lane = _iota((t, LANES), 1)
    row = _iota((t, t), 0)
    col = _iota((t, t), 1)
    strict = col < row
    tri = (row > col).astype(BF16)
    q2 = q_ref[0]
    q0 = pl.multiple_of(qi * t, t)

    def tile(qh, k2, v2, carry):
        z = _dot_nt(qh, k2)
        ls = _log_sigmoid(z)
        log1m = ls - z
        if carry is None:
            log1m = jnp.where(strict, log1m, 0.0)
        after = _dot_xr(log1m, tri, 1)
        if carry is not None:
            after = after + carry
        attn = jnp.exp(ls + after)
        if carry is None:
            attn = jnp.where(strict, attn, 0.0)
        return _dot(attn, v2), jnp.sum(log1m, axis=-1, keepdims=True)

    def any_live(carries):
        m = carries[0]
        for c in carries[1:]:
            m = jnp.maximum(m, c)
        return (jnp.max(m) > EXP_ZERO).astype(jnp.int32)

    def run(heads):
        qhs = [jnp.where(_head_lanes(lane, h), q2, jnp.zeros_like(q2)) for h in heads]
        k2 = k_ref[0, pl.ds(q0, t), :]
        v2 = v_ref[0, pl.ds(q0, t), :]
        first = [tile(qh, k2, v2, None) for qh in qhs]
        accs = tuple(f[0] for f in first)
        carries = tuple(f[1] for f in first)

        def cond(st):
            return (st[0] < qi) & (st[1] > 0)

        def body(st):
            j, _, accs, carries = st
            k0 = pl.multiple_of((qi - 1 - j) * t, t)
            k2 = k_ref[0, pl.ds(k0, t), :]
            v2 = v_ref[0, pl.ds(k0, t), :]
            new = [tile(qh, k2, v2, c) for qh, c in zip(qhs, carries)]
            accs = tuple(a + n[0] for a, n in zip(accs, new))
            carries = tuple(c + n[1] for c, n in zip(carries, new))
            return j + 1, any_live(carries), accs, carries

        st = lax.while_loop(cond, body, (jnp.int32(0), any_live(carries), accs, carries))
        accs = st[2]
        hi = accs[1] if len(heads) == 2 else jnp.zeros_like(accs[0])
        o_ref[0] = jnp.where(lane < HEAD_DIM, accs[0], hi)

    @pl.when(pi < n_single)
    def _():
        run((0, 1))

    @pl.when(pi >= n_single)
    def _():
        run((0,))


def _sb_attention(q, k, v, n_pairs, t=256):
    b, s, _ = q.shape
    return pl.pallas_call(
        functools.partial(_sb_kernel, t=t, n_single=n_pairs - 1),
        grid=(b, n_pairs, s // t),
        in_specs=[pl.BlockSpec((1, t, LANES), lambda bi, pi, qi: (bi, qi, pi)),
                  pl.BlockSpec((1, s, LANES), lambda bi, pi, qi: (bi, 0, pi)),
                  pl.BlockSpec((1, s, LANES), lambda bi, pi, qi: (bi, 0, pi))],
        out_specs=pl.BlockSpec((1, t, LANES), lambda bi, pi, qi: (bi, qi, pi)),
        out_shape=jax.ShapeDtypeStruct((b, s, n_pairs * LANES), F32),
        compiler_params=_cparams(("parallel", "parallel", "arbitrary")),
        name="stickbreak",
    )(q, k, v)


def _fox_kernel(q_ref, k_ref, v_ref, ccol_ref, crow_ref, o_ref, *, t):
    pi = pl.program_id(1)
    qi = pl.program_id(2)
    lane = _iota((t, LANES), 1)
    causal = _iota((t, t), 1) <= _iota((t, t), 0)
    q2 = q_ref[0]
    q0 = pl.multiple_of(qi * t, t)

    def run(heads):
        cis = [2 * pi + h for h in heads]
        qhs = [jnp.where(_head_lanes(lane, h), q2, jnp.zeros_like(q2)) for h in heads]
        cqs = [jnp.sum(jnp.where(lane == ci, ccol_ref[0], 0.0), axis=-1, keepdims=True)
               for ci in cis]

        def scores(i, k2, k0):
            ck = crow_ref[0, pl.ds(cis[i], 1), pl.ds(k0, t)]
            return _dot_nt(qhs[i], k2) + cqs[i] - ck

        def update(st, s_blk, v2):
            m, l, acc = st
            m_new = jnp.maximum(m, jnp.max(s_blk, axis=-1, keepdims=True))
            alpha = jnp.exp(m - m_new)
            p = jnp.exp(s_blk - m_new)
            l = alpha * l + jnp.sum(p, axis=-1, keepdims=True)
            acc = alpha * acc + _dot(p, v2)
            return m_new, l, acc

        def body(j, sts):
            k0 = pl.multiple_of(j * t, t)
            k2 = k_ref[0, pl.ds(k0, t), :]
            v2 = v_ref[0, pl.ds(k0, t), :]
            return tuple(update(st, scores(i, k2, k0), v2) for i, st in enumerate(sts))

        init = tuple((jnp.full((t, 1), -1e30, F32), jnp.zeros((t, 1), F32),
                      jnp.zeros((t, LANES), F32)) for _ in heads)
        sts = lax.fori_loop(0, qi, body, init)
        k2 = k_ref[0, pl.ds(q0, t), :]
        v2 = v_ref[0, pl.ds(q0, t), :]
        outs = []
        for i, st in enumerate(sts):
            m, l, acc = update(st, jnp.where(causal, scores(i, k2, q0), -1e30), v2)
            outs.append(acc * (1.0 / l))
        lo = outs[0] if len(heads) == 2 else jnp.zeros_like(outs[0])
        o_ref[0] = jnp.where(lane < HEAD_DIM, lo, outs[-1])

    @pl.when(pi > 0)
    def _():
        run((0, 1))

    @pl.when(pi == 0)
    def _():
        run((1,))


def _fox_attention(q, k, v, ccol, crow, first_pair, n_pairs, t=512):
    b, s, _ = q.shape
    return pl.pallas_call(
        functools.partial(_fox_kernel, t=t),
        grid=(b, n_pairs, s // t),
        in_specs=[pl.BlockSpec((1, t, LANES), lambda bi, pi, qi: (bi, qi, pi + first_pair)),
                  pl.BlockSpec((1, s, LANES), lambda bi, pi, qi: (bi, 0, pi + first_pair)),
                  pl.BlockSpec((1, s, LANES), lambda bi, pi, qi: (bi, 0, pi + first_pair)),
                  pl.BlockSpec((1, t, LANES), lambda bi, pi, qi: (bi, qi, 0)),
                  pl.BlockSpec((1, 8, s), lambda bi, pi, qi: (bi, 0, 0))],
        out_specs=pl.BlockSpec((1, t, LANES), lambda bi, pi, qi: (bi, qi, pi)),
        out_shape=jax.ShapeDtypeStruct((b, s, n_pairs * LANES), F32),
        compiler_params=_cparams(("parallel", "parallel", "arbitrary")),
        name="fox",
    )(q, k, v, ccol, crow)


def _unit_lower_inverses(lbds, c):
    n = lbds[0].shape[0]
    row = _iota((n, n), 0)
    col = _iota((n, n), 1)
    eye = (row == col).astype(F32)
    same2 = _blk(row, 2) == _blk(col, 2)
    xs = [eye + jnp.where(same2, l, 0.0) for l in lbds]
    s = 2
    while s < c:
        join = (_blk(row, 2 * s) == _blk(col, 2 * s)) & (_blk(row, s) != _blk(col, s))
        ts = [_dot(jnp.where(join, l, 0.0), x) for l, x in zip(lbds, xs)]
        xs = [x + _dot(x, t) for x, t in zip(xs, ts)]
        s *= 2
    return xs


def _rwkv_kernel(pa_ref, mu_ref, wl_ref, w0_ref, a0_ref, kk_ref, ka_ref, rk_ref, lnw_ref, lnb_ref,
                 o_ref, prev_ref, h_ref, *, tb, c, da):
    ti = pl.program_id(1)
    n_pairs = da // LANES
    n_chunks = tb // c
    width = pa_ref.shape[-1]
    assert 2 * c == LANES

    @pl.when(ti == 0)
    def _():
        prev_ref[...] = jnp.zeros_like(prev_ref)
        h_ref[...] = jnp.zeros_like(h_ref)

    p = pa_ref[0]
    prow = _iota((tb, width), 0)
    prev = jnp.where(prow == 0, prev_ref[...], pltpu.roll(p, 1, 0))
    prev_ref[...] = p[tb - 1:tb, :]
    xs = p + mu_ref[...] * (prev - p)
    r = xs[:, 0:da]
    k = xs[:, da:2 * da]
    v = xs[:, 2 * da:3 * da]
    wa = xs[:, 3 * da:3 * da + LANES]
    lane_t = _iota((tb, LANES), 1)
    wa = jnp.where(lane_t < HEAD_DIM, jnp.tanh(wa), wa)
    lora = _dot(wa, wl_ref[...])
    wpre = w0_ref[...] + lora[:, 0:da]
    w = -(jnp.maximum(-wpre, 0.0) + jnp.log1p(jnp.exp(-jnp.abs(wpre)))) - 0.5
    lw = -jnp.exp(w)
    a = _sigmoid(a0_ref[...] + lora[:, da:2 * da])
    k2 = k * (1.0 + (a - 1.0) * ka_ref[...])
    kk = k * kk_ref[...]

    er = _blk(_iota((LANES, LANES), 0), HEAD_DIM)
    ec = _blk(_iota((LANES, LANES), 1), HEAD_DIM)
    bd_mask = er == ec
    ones_head = bd_mask.astype(BF16)
    diag_eye = (_iota((LANES, LANES), 0) == _iota((LANES, LANES), 1)).astype(F32)

    pairs = [slice(i * LANES, (i + 1) * LANES) for i in range(n_pairs)]
    kkn = jnp.concatenate(
        [kk[:, sl] * lax.rsqrt(jnp.maximum(_dot_xr(kk[:, sl] * kk[:, sl], ones_head, 2), 1e-24))
         for sl in pairs], axis=1)
    av = -kkn
    bv = kkn * a

    tr = _iota((tb, tb), 0)
    tc = _iota((tb, tb), 1)
    tri_blk = ((_blk(tr, c) == _blk(tc, c)) & (tc <= tr)).astype(BF16)
    cum = _dot_lx(tri_blk, lw, 3)
    eg = jnp.exp(cum)
    rt = r * eg
    at = av * jnp.exp(cum - lw)
    ieg = jnp.exp(-cum)
    kt = k2 * ieg
    bt = bv * ieg

    rr = _iota((2 * c, 2 * c), 0)
    cc = _iota((2 * c, 2 * c), 1)
    same_blk = _blk(rr, c) == _blk(cc, c)
    strict_2c = _rem(cc, c) < _rem(rr, c)
    incl_2c = _rem(cc, c) <= _rem(rr, c)
    m0 = _iota((c, LANES), 1) < HEAD_DIM
    m0w = _rem(_iota((c, 2 * LANES), 1), LANES) < HEAD_DIM
    zeros_c = jnp.zeros((c, LANES), F32)

    items = [(sl, slice(j * c, (j + 1) * c)) for sl in pairs for j in range(n_chunks)]
    lbds, laks, prms, qa2s, vstk, gams, bkbars = [], [], [], [], [], [], []
    for sl, rs in items:
        rc, ac, kc, bc, vc = rt[rs, sl], at[rs, sl], kt[rs, sl], bt[rs, sl], v[rs, sl]
        gam = eg[rs.stop - 1:rs.stop, sl]
        bk = jnp.concatenate([bc, kc], axis=0)
        at0 = jnp.where(m0, ac, 0.0)
        at1 = jnp.where(m0, 0.0, ac)
        qa4 = jnp.concatenate([at0, at1, jnp.where(m0, rc, 0.0), jnp.where(m0, 0.0, rc)], axis=0)
        pm = _dot_nt(qa4, bk)
        xall = jnp.concatenate([pm[0:c], pltpu.roll(pm[c:2 * c], c, 1)], axis=0)
        lbds.append(jnp.where(same_blk & strict_2c, xall, 0.0))
        laks.append(jnp.where((~same_blk) & strict_2c, xall, 0.0))
        prms.append(jnp.where(incl_2c, pm[2 * c:], 0.0))
        qa2s.append(jnp.concatenate([at0, at1], axis=0))
        vstk.append(jnp.concatenate([jnp.where(m0, 0.0, vc), jnp.where(m0, vc, 0.0)], axis=0))
        gams.append(gam)
        bkbars.append(jnp.concatenate([bc * gam, kc * gam], axis=0))

    tinvs = _unit_lower_inverses(lbds, c)
    w1s = [_dot(lak, vs) for lak, vs in zip(laks, vstk)]
    tzs = [_dot(tinv, jnp.concatenate([qa2, w1], axis=1))
           for tinv, qa2, w1 in zip(tinvs, qa2s, w1s)]
    r3s, y2s, phis, gs = [], [], [], []
    for (sl, rs), tz, prm, bkbar, gam in zip(items, tzs, prms, bkbars, gams):
        a2v2 = tz[0:c] + tz[c:2 * c]
        rhs = jnp.concatenate([a2v2, jnp.concatenate([zeros_c, v[rs, sl]], axis=1)], axis=0)
        res = _dot(jnp.concatenate([prm, bkbar.T], axis=0), rhs)
        ry = jnp.where(m0w, res[0:c], res[c:2 * c])
        r3s.append(rt[rs, sl] + ry[:, 0:LANES])
        y2s.append(ry[:, LANES:])
        pg = res[2 * c:]
        phis.append(diag_eye * gam + jnp.where(bd_mask, pg[:, 0:LANES], 0.0))
        gs.append(jnp.where(bd_mask, pg[:, LANES:], 0.0))

    hs = [h_ref[i] for i in range(n_pairs)]
    ys = [[] for _ in range(n_pairs)]
    for j in range(n_chunks):
        for i in range(n_pairs):
            idx = i * n_chunks + j
            ys[i].append(_dot(r3s[idx], hs[i]) + y2s[idx])
        hs = [_dot(phis[i * n_chunks + j], hs[i]) + gs[i * n_chunks + j] for i in range(n_pairs)]
    for i, sl in enumerate(pairs):
        h_ref[i] = hs[i]
        y = jnp.concatenate(ys[i], axis=0)
        mean = _dot_xr(y, ones_head, 3) * (1.0 / HEAD_DIM)
        yc_ = y - mean
        var = _dot_xr(yc_ * yc_, ones_head, 3) * (1.0 / HEAD_DIM)
        yn = yc_ * lax.rsqrt(var + GN_EPS) * lnw_ref[:, sl] + lnb_ref[:, sl]
        bonus = _dot_xr(r[:, sl] * k2[:, sl] * rk_ref[:, sl], ones_head, 3) * v[:, sl]
        o_ref[0, :, sl] = yn + bonus


def _rwkv(pa, mu, wl, w0, a0, k_k, k_a, r_k, ln_w, ln_b, tb=256, c=64):
    b, s, width = pa.shape
    da = w0.shape[-1]
    vec = lambda n: pl.BlockSpec((1, n), lambda bi, ti: (0, 0))
    return pl.pallas_call(
        functools.partial(_rwkv_kernel, tb=tb, c=c, da=da),
        grid=(b, s // tb),
        in_specs=[pl.BlockSpec((1, tb, width), lambda bi, ti: (bi, ti, 0)),
                  vec(width),
                  pl.BlockSpec(wl.shape, lambda bi, ti: (0, 0)),
                  vec(da), vec(da), vec(da), vec(da), vec(da), vec(da), vec(da)],
        out_specs=pl.BlockSpec((1, tb, da), lambda bi, ti: (bi, ti, 0)),
        out_shape=jax.ShapeDtypeStruct((b, s, da), F32),
        scratch_shapes=[pltpu.VMEM((1, width), F32),
                        pltpu.VMEM((da // LANES, LANES, LANES), F32)],
        compiler_params=_cparams(("parallel", "arbitrary")),
        name="rwkv7",
    )(pa, mu, wl, w0, a0, k_k, k_a, r_k, ln_w, ln_b)


def kernel(x, norm_w, w_in, b_f, mu, w0, w_up, a0, a_up, k_k, k_a, r_k, ln_x_w, ln_x_b, w_out,
           final_norm_w):
    b, s, d = x.shape
    depth = w_in.shape[0]
    da = w0.shape[-1]
    lora = w_up.shape[1]
    hc = b_f.shape[-1]
    d_mix = w_out.shape[1]
    db = (d_mix - da) // 2
    n_shift = 3 * da + 2 * lora
    assert da % LANES == 0 and 2 * lora == LANES and db == hc * HEAD_DIM and hc % 2 == 1
    assert mu.shape[-1] == n_shift
    n_pairs_attn = (db + HEAD_DIM) // LANES
    dbp = n_pairs_attn * LANES
    scale = HEAD_DIM ** -0.5

    x2 = x.reshape(b * s, d)
    zpad = lambda rows, n: jnp.zeros((rows, n), F32)
    for l in range(depth):
        wl_in = w_in[l]
        o = n_shift
        qb, kb, vb = (wl_in[:, o + i * db:o + (i + 1) * db] for i in range(3))
        o += 3 * db
        qc, kc, vc = (wl_in[:, o + i * db:o + (i + 1) * db] for i in range(3))
        o += 3 * db
        wf = wl_in[:, o:o + hc]
        wg = wl_in[:, o + hc:]
        w_all = jnp.concatenate(
            [wl_in[:, :n_shift],
             qb * scale, qc * scale, kb, kc, vb, vc,
             zpad(d, 1), wf, zpad(d, LANES - 1 - hc),
             wg[:, :da + db], zpad(d, 2 * HEAD_DIM), wg[:, da + db:]], axis=1).astype(BF16)
        widths = (n_shift, 2 * db, 2 * db, 2 * db, LANES, da + 2 * dbp)
        dtypes = (F32, BF16, BF16, BF16, F32, F32)
        pa, q, k, v, f, g = _inproj(x2, norm_w[l][None, :], w_all, widths, dtypes)

        wl_comb = jnp.concatenate(
            [jnp.concatenate([w_up[l], zpad(lora, da)], axis=1),
             jnp.concatenate([zpad(lora, da), a_up[l]], axis=1)], axis=0).astype(BF16)
        row = lambda t: t.reshape(1, -1)
        ya = _rwkv(pa.reshape(b, s, n_shift), row(mu[l]), wl_comb, row(w0[l]), row(a0[l]),
                   row(k_k[l]), row(k_a[l]), row(r_k[l]), row(ln_x_w[l]), row(ln_x_b[l]))

        q3, k3, v3 = (t.reshape(b, s, 2 * db) for t in (q, k, v))
        yb = _sb_attention(q3, k3, v3, n_pairs_attn)
        bf_pad = jnp.concatenate([jnp.zeros((1,), F32), b_f[l], jnp.zeros((LANES - 1 - hc,), F32)])
        ccol, crow = _fcum(f.reshape(b, s, LANES), bf_pad[None, :])
        yc = _fox_attention(q3, k3, v3, ccol, crow, first_pair=n_pairs_attn - 1, n_pairs=n_pairs_attn)

        wo = w_out[l]
        wo_p = jnp.concatenate([wo[:da + db], zpad(2 * HEAD_DIM, d), wo[da + db:]], axis=0).astype(BF16)
        x2 = _outproj(x2, ya.reshape(b * s, da), yb.reshape(b * s, dbp), yc.reshape(b * s, dbp),
                      g, wo_p, final_norm_w[None, :], final=(l == depth - 1))
    return x2.reshape(b, s, d)
```

```python
import functools

import jax
import jax.numpy as jnp
from jax import lax
from jax.experimental import pallas as pl
from jax.experimental.pallas import tpu as pltpu

HEAD_DIM = 64
LANES = 128
NORM_EPS = 1e-6
GN_EPS = 64e-5
VMEM_LIMIT = 56 * 1024 * 1024
EXP_ZERO = -104.0

F32 = jnp.float32
BF16 = jnp.bfloat16


def _iota(shape, dim):
    return lax.broadcasted_iota(jnp.int32, shape, dim)


def _blk(idx, size):
    assert size & (size - 1) == 0
    return jnp.right_shift(idx, size.bit_length() - 1)


def _rem(idx, size):
    assert size & (size - 1) == 0
    return jnp.bitwise_and(idx, size - 1)


def _dot(a, b):
    return jnp.dot(a.astype(BF16), b.astype(BF16), preferred_element_type=F32)


def _dot_nt(a, b):
    return lax.dot_general(a.astype(BF16), b.astype(BF16), (((1,), (1,)), ((), ())),
                           preferred_element_type=F32)


def _split(x, n):
    parts = []
    rem = x
    for i in range(n):
        p = rem.astype(BF16)
        parts.append(p)
        if i + 1 < n:
            rem = rem - p.astype(F32)
    return parts


def _dot_xr(x, r, n):
    out = None
    for p in _split(x, n):
        t = jnp.dot(p, r, preferred_element_type=F32)
        out = t if out is None else out + t
    return out


def _dot_lx(l, x, n):
    out = None
    for p in _split(x, n):
        t = jnp.dot(l, p, preferred_element_type=F32)
        out = t if out is None else out + t
    return out


def _log_sigmoid(z):
    return jnp.minimum(z, 0.0) - jnp.log1p(jnp.exp(-jnp.abs(z)))


def _sigmoid(z):
    return 1.0 / (1.0 + jnp.exp(-z))


def _head_lanes(lane, h):
    return (lane >= h * HEAD_DIM) & (lane < (h + 1) * HEAD_DIM)


def _cparams(sem):
    return pltpu.CompilerParams(dimension_semantics=sem, vmem_limit_bytes=VMEM_LIMIT)


def _inproj_kernel(x_ref, nw_ref, w_ref, pa_ref, q_ref, k_ref, v_ref, f_ref, g_ref, *, n_chunk):
    x = x_ref[...]
    xn = x * lax.rsqrt(jnp.mean(x * x, axis=-1, keepdims=True) + NORM_EPS)
    h = (xn * nw_ref[...]).astype(BF16)
    off = 0
    for ref in (pa_ref, q_ref, k_ref, v_ref, f_ref, g_ref):
        width = ref.shape[-1]
        for c0 in range(0, width, n_chunk):
            c1 = min(c0 + n_chunk, width)
            ref[:, c0:c1] = jnp.dot(h, w_ref[:, off + c0:off + c1],
                                    preferred_element_type=F32).astype(ref.dtype)
        off += width


def _inproj(x2, nw, w, widths, dtypes, tm=512, n_chunk=640):
    m, d = x2.shape
    n = w.shape[1]
    assert sum(widths) == n and m % tm == 0
    out_shape = [jax.ShapeDtypeStruct((m, wd), dt) for wd, dt in zip(widths, dtypes)]
    out_specs = [pl.BlockSpec((tm, wd), lambda i: (i, 0)) for wd in widths]
    return pl.pallas_call(
        functools.partial(_inproj_kernel, n_chunk=n_chunk),
        grid=(m // tm,),
        in_specs=[pl.BlockSpec((tm, d), lambda i: (i, 0)),
                  pl.BlockSpec((1, d), lambda i: (0, 0)),
                  pl.BlockSpec((d, n), lambda i: (0, 0))],
        out_specs=out_specs,
        out_shape=out_shape,
        compiler_params=_cparams(("parallel",)),
        name="inproj",
    )(x2, nw, w)


def _outproj_kernel(x_ref, ya_ref, yb_ref, yc_ref, g_ref, w_ref, fw_ref, o_ref, *, final):
    da = ya_ref.shape[-1]
    db = yb_ref.shape[-1]
    acc = x_ref[...]
    off = 0
    for y_ref, wd in ((ya_ref, da), (yb_ref, db), (yc_ref, db)):
        g = g_ref[:, off:off + wd].astype(F32)
        yg = y_ref[...].astype(F32) * (g * _sigmoid(g))
        acc = acc + jnp.dot(yg.astype(BF16), w_ref[off:off + wd, :], preferred_element_type=F32)
        off += wd
    if final:
        acc = acc * lax.rsqrt(jnp.mean(acc * acc, axis=-1, keepdims=True) + NORM_EPS) * fw_ref[...]
    o_ref[...] = acc


def _outproj(x2, ya, yb, yc, g, w, fw, final, tm=512):
    m, d = x2.shape
    row = lambda wd: pl.BlockSpec((tm, wd), lambda i: (i, 0))
    return pl.pallas_call(
        functools.partial(_outproj_kernel, final=final),
        grid=(m // tm,),
        in_specs=[row(d), row(ya.shape[1]), row(yb.shape[1]), row(yc.shape[1]), row(g.shape[1]),
                  pl.BlockSpec(w.shape, lambda i: (0, 0)),
                  pl.BlockSpec((1, d), lambda i: (0, 0))],
        out_specs=row(d),
        out_shape=jax.ShapeDtypeStruct((m, d), F32),
        compiler_params=_cparams(("parallel",)),
        name="outproj",
    )(x2, ya, yb, yc, g, w, fw)


def _fcum_kernel(f_ref, bf_ref, c_ref, *, blk):
    s = f_ref.shape[1]
    row = _iota((blk, blk), 0)
    col = _iota((blk, blk), 1)
    tri = (col <= row).astype(BF16)
    carry = jnp.zeros((1, LANES), F32)
    for i in range(s // blk):
        lf = _log_sigmoid(f_ref[0, i * blk:(i + 1) * blk, :] + bf_ref[...])
        c = _dot_lx(tri, lf, 3) + carry
        c_ref[0, i * blk:(i + 1) * blk, :] = c
        carry = c[blk - 1:blk, :]


def _fcum(f, bf, blk=256):
    b, s, _ = f.shape
    return pl.pallas_call(
        functools.partial(_fcum_kernel, blk=blk),
        grid=(b,),
        in_specs=[pl.BlockSpec((1, s, LANES), lambda i: (i, 0, 0)),
                  pl.BlockSpec((1, LANES), lambda i: (0, 0))],
        out_specs=pl.BlockSpec((1, s, LANES), lambda i: (i, 0, 0)),
        out_shape=jax.ShapeDtypeStruct((b, s, LANES), F32),
        compiler_params=_cparams(("parallel",)),
        name="fcum",
    )(f, bf)


def _sb_kernel(q_ref, k_ref, v_ref, o_ref, *, t, heads):
    qi = pl.program_id(1)
    lane = _iota((t, LANES), 1)
    row = _iota((t, t), 0)
    col = _iota((t, t), 1)
    strict = col < row
    tri = (row > col).astype(BF16)
    q0 = pl.multiple_of(qi * t, t)
    p0 = pl.multiple_of(jnp.maximum(qi - 1, 0) * t, t)
    no_prev = jnp.where(qi > 0, 0.0, -1e30).astype(F32)

    def kv(k0, blk):
        sl = slice(blk * LANES, (blk + 1) * LANES)
        return k_ref[0, pl.ds(k0, t), sl], v_ref[0, pl.ds(k0, t), sl]

    def logits(qh, k2, diag):
        z = _dot_nt(qh, k2)
        ls = jnp.minimum(z, 0.0) - jnp.log(1.0 + jnp.exp(-jnp.abs(z)))
        log1m = ls - z
        if diag:
            log1m = jnp.where(strict, log1m, 0.0)
        return ls, _dot_xr(log1m, tri, 1), jnp.sum(log1m, axis=-1, keepdims=True)

    qhs, accs, carries = [], [], []
    for blk, half in heads:
        q2 = q_ref[0, :, blk * LANES:(blk + 1) * LANES]
        qh = jnp.where(_head_lanes(lane, half), q2, jnp.zeros_like(q2))
        kd, vd = kv(q0, blk)
        kp, vp = kv(p0, blk)
        ls_d, after_d, mass_d = logits(qh, kd, True)
        ls_p, after_p, mass_p = logits(qh, kp, False)
        attn_d = jnp.where(strict, jnp.exp(ls_d + after_d), 0.0)
        attn_p = jnp.exp(ls_p + after_p + (mass_d + no_prev))
        qhs.append(qh)
        accs.append(_dot(attn_d, vd) + _dot(attn_p, vp))
        carries.append(mass_d + mass_p)

    def any_live(carries):
        m = carries[0]
        for c in carries[1:]:
            m = jnp.maximum(m, c)
        return (jnp.max(m) > EXP_ZERO).astype(jnp.int32)

    def cond(st):
        return (st[0] >= 0) & (st[1] > 0)

    def body(st):
        kb, _, accs, carries = st
        k0 = pl.multiple_of(kb * t, t)
        new_accs, new_carries = [], []
        for (blk, _), qh, acc, carry in zip(heads, qhs, accs, carries):
            k2, v2 = kv(k0, blk)
            ls, after, mass = logits(qh, k2, False)
            new_accs.append(acc + _dot(jnp.exp(ls + after + carry), v2))
            new_carries.append(carry + mass)
        return kb - 1, any_live(new_carries), tuple(new_accs), tuple(new_carries)

    st = lax.while_loop(cond, body, (qi - 2, any_live(carries), tuple(accs), tuple(carries)))
    accs = st[2]
    for blk in sorted({b for b, _ in heads}):
        halves = {h: a for (b, h), a in zip(heads, accs) if b == blk}
        zero = jnp.zeros((t, LANES), F32)
        o_ref[0, :, blk * LANES:(blk + 1) * LANES] = jnp.where(
            lane < HEAD_DIM, halves.get(0, zero), halves.get(1, zero)).astype(o_ref.dtype)


def _sb_attention(q, k, v, heads, n_blocks, t=256):
    b, s, w = q.shape
    return pl.pallas_call(
        functools.partial(_sb_kernel, t=t, heads=heads),
        grid=(b, s // t),
        in_specs=[pl.BlockSpec((1, t, w), lambda bi, qi: (bi, qi, 0)),
                  pl.BlockSpec((1, s, w), lambda bi, qi: (bi, 0, 0)),
                  pl.BlockSpec((1, s, w), lambda bi, qi: (bi, 0, 0))],
        out_specs=pl.BlockSpec((1, t, n_blocks * LANES), lambda bi, qi: (bi, qi, 0)),
        out_shape=jax.ShapeDtypeStruct((b, s, n_blocks * LANES), BF16),
        compiler_params=_cparams(("parallel", "arbitrary")),
        name="stickbreak",
    )(q, k, v)


def _bias_lanes(lane, base, own, vals, ones_first):
    slots = ([None] * 3 + list(vals)) if ones_first else (list(vals) + [None] * 3)
    out = own
    for i, val in enumerate(slots):
        fill = jnp.ones((), BF16) if val is None else val
        out = jnp.where(lane == base + i, fill, out)
    return out


def _fox_kernel(q_ref, k_ref, v_ref, cq_ref, ck_ref, o_ref, kaug_ref, *, t, heads, first_blk):
    qi = pl.program_id(1)
    s_len = k_ref.shape[1]
    lane = _iota((t, LANES), 1)
    causal = _iota((t, t), 1) <= _iota((t, t), 0)
    q0 = pl.multiple_of(qi * t, t)

    @pl.when(qi == 0)
    def _():
        lane_s = _iota((s_len, LANES), 1)
        call = ck_ref[0]
        for i, (blk, half, ci) in enumerate(heads):
            ck = jnp.sum(jnp.where(lane_s == ci, call, 0.0), axis=-1, keepdims=True)
            kaug_ref[i] = _bias_lanes(lane_s, HEAD_DIM * (1 - half),
                                      k_ref[0, :, blk * LANES:(blk + 1) * LANES],
                                      [-p for p in _split(ck, 3)], ones_first=True)

    qas = []
    for blk, half, ci in heads:
        q2 = q_ref[0, :, blk * LANES:(blk + 1) * LANES]
        cq = jnp.sum(jnp.where(lane == ci, cq_ref[0], 0.0), axis=-1, keepdims=True)
        own = jnp.where(_head_lanes(lane, half), q2, jnp.zeros_like(q2))
        qas.append(_bias_lanes(lane, HEAD_DIM * (1 - half), own, _split(cq, 3), ones_first=False))

    def update(st, s_blk, v2):
        m, l, acc = st
        m_new = jnp.maximum(m, jnp.max(s_blk, axis=-1, keepdims=True))
        alpha = jnp.exp(m - m_new)
        p = jnp.exp(s_blk - m_new)
        l = alpha * l + jnp.sum(p, axis=-1, keepdims=True)
        acc = alpha * acc + _dot(p, v2)
        return m_new, l, acc

    def sweep(k0, sts, diag):
        new = []
        for i, ((blk, _, _), st) in enumerate(zip(heads, sts)):
            s_blk = _dot_nt(qas[i], kaug_ref[i, pl.ds(k0, t), :])
            if diag:
                s_blk = jnp.where(causal, s_blk, -1e30)
            v2 = v_ref[0, pl.ds(k0, t), blk * LANES:(blk + 1) * LANES]
            new.append(update(st, s_blk, v2))
        return tuple(new)

    init = tuple((jnp.full((t, 1), -1e30, F32), jnp.zeros((t, 1), F32),
                  jnp.zeros((t, LANES), F32)) for _ in heads)
    sts = lax.fori_loop(0, qi, lambda j, sts: sweep(pl.multiple_of(j * t, t), sts, False), init)
    sts = sweep(q0, sts, True)
    outs = [acc * (1.0 / l) for _, l, acc in sts]
    for blk in sorted({b for b, _, _ in heads}):
        halves = {h: o for (b, h, _), o in zip(heads, outs) if b == blk}
        zero = jnp.zeros((t, LANES), F32)
        ob = blk - first_blk
        o_ref[0, :, ob * LANES:(ob + 1) * LANES] = jnp.where(
            lane < HEAD_DIM, halves.get(0, zero), halves.get(1, zero)).astype(o_ref.dtype)


def _fox_attention(q, k, v, c, heads, first_blk, n_blocks, t=512):
    b, s, w = q.shape
    return pl.pallas_call(
        functools.partial(_fox_kernel, t=t, heads=heads, first_blk=first_blk),
        grid=(b, s // t),
        in_specs=[pl.BlockSpec((1, t, w), lambda bi, qi: (bi, qi, 0)),
                  pl.BlockSpec((1, s, w), lambda bi, qi: (bi, 0, 0)),
                  pl.BlockSpec((1, s, w), lambda bi, qi: (bi, 0, 0)),
                  pl.BlockSpec((1, t, LANES), lambda bi, qi: (bi, qi, 0)),
                  pl.BlockSpec((1, s, LANES), lambda bi, qi: (bi, 0, 0))],
        out_specs=pl.BlockSpec((1, t, n_blocks * LANES), lambda bi, qi: (bi, qi, 0)),
        out_shape=jax.ShapeDtypeStruct((b, s, n_blocks * LANES), BF16),
        scratch_shapes=[pltpu.VMEM((len(heads), s, LANES), BF16)],
        compiler_params=_cparams(("parallel", "arbitrary")),
        name="fox",
    )(q, k, v, c, c)


def _unit_lower_inverses(lbds, c):
    n = lbds[0].shape[0]
    row = _iota((n, n), 0)
    col = _iota((n, n), 1)
    eye = (row == col).astype(F32)
    same2 = _blk(row, 2) == _blk(col, 2)
    xs = [eye + jnp.where(same2, l, 0.0) for l in lbds]
    s = 2
    while s < c:
        join = (_blk(row, 2 * s) == _blk(col, 2 * s)) & (_blk(row, s) != _blk(col, s))
        ts = [_dot(jnp.where(join, l, 0.0), x) for l, x in zip(lbds, xs)]
        xs = [x + _dot(x, t) for x, t in zip(xs, ts)]
        s *= 2
    return xs


def _rwkv_kernel(pa_ref, mu_ref, wl_ref, w0_ref, a0_ref, kk_ref, ka_ref, rk_ref, lnw_ref, lnb_ref,
                 o_ref, prev_ref, h_ref, *, tb, c, da):
    ti = pl.program_id(1)
    n_pairs = da // LANES
    n_chunks = tb // c
    width = pa_ref.shape[-1]
    assert 2 * c == LANES

    @pl.when(ti == 0)
    def _():
        prev_ref[...] = jnp.zeros_like(prev_ref)
        h_ref[...] = jnp.zeros_like(h_ref)

    p = pa_ref[0]
    prow = _iota((tb, width), 0)
    prev = jnp.where(prow == 0, prev_ref[...], pltpu.roll(p, 1, 0))
    prev_ref[...] = p[tb - 1:tb, :]
    xs = p + mu_ref[...] * (prev - p)
    r = xs[:, 0:da]
    k = xs[:, da:2 * da]
    v = xs[:, 2 * da:3 * da]
    wa = xs[:, 3 * da:3 * da + LANES]
    lane_t = _iota((tb, LANES), 1)
    wa = jnp.where(lane_t < HEAD_DIM, jnp.tanh(wa), wa)
    lora = _dot(wa, wl_ref[...])
    wpre = w0_ref[...] + lora[:, 0:da]
    w = -(jnp.maximum(-wpre, 0.0) + jnp.log1p(jnp.exp(-jnp.abs(wpre)))) - 0.5
    lw = -jnp.exp(w)
    a = _sigmoid(a0_ref[...] + lora[:, da:2 * da])
    k2 = k * (1.0 + (a - 1.0) * ka_ref[...])
    kk = k * kk_ref[...]

    er = _blk(_iota((LANES, LANES), 0), HEAD_DIM)
    ec = _blk(_iota((LANES, LANES), 1), HEAD_DIM)
    bd_mask = er == ec
    ones_head = bd_mask.astype(BF16)
    diag_eye = (_iota((LANES, LANES), 0) == _iota((LANES, LANES), 1)).astype(F32)

    pairs = [slice(i * LANES, (i + 1) * LANES) for i in range(n_pairs)]
    kkn = jnp.concatenate(
        [kk[:, sl] * lax.rsqrt(jnp.maximum(_dot_xr(kk[:, sl] * kk[:, sl], ones_head, 2), 1e-24))
         for sl in pairs], axis=1)
    av = -kkn
    bv = kkn * a

    tr = _iota((tb, tb), 0)
    tc = _iota((tb, tb), 1)
    tri_blk = ((_blk(tr, c) == _blk(tc, c)) & (tc <= tr)).astype(BF16)
    cum = _dot_lx(tri_blk, lw, 2)
    eg = jnp.exp(cum)
    rt = r * eg
    at = av * jnp.exp(cum - lw)
    ieg = jnp.exp(-cum)
    kt = k2 * ieg
    bt = bv * ieg

    rr = _iota((2 * c, 2 * c), 0)
    cc = _iota((2 * c, 2 * c), 1)
    same_blk = _blk(rr, c) == _blk(cc, c)
    strict_2c = _rem(cc, c) < _rem(rr, c)
    incl_2c = _rem(cc, c) <= _rem(rr, c)
    m0 = _iota((c, LANES), 1) < HEAD_DIM
    m0w = _rem(_iota((c, 2 * LANES), 1), LANES) < HEAD_DIM
    zeros_c = jnp.zeros((c, LANES), F32)

    items = [(sl, slice(j * c, (j + 1) * c)) for sl in pairs for j in range(n_chunks)]
    lbds, laks, prms, qa2s, vstk, gams, bkbars = [], [], [], [], [], [], []
    for sl, rs in items:
        rc, ac, kc, bc, vc = rt[rs, sl], at[rs, sl], kt[rs, sl], bt[rs, sl], v[rs, sl]
        gam = eg[rs.stop - 1:rs.stop, sl]
        bk = jnp.concatenate([bc, kc], axis=0)
        at0 = jnp.where(m0, ac, 0.0)
        at1 = jnp.where(m0, 0.0, ac)
        qa4 = jnp.concatenate([at0, at1, jnp.where(m0, rc, 0.0), jnp.where(m0, 0.0, rc)], axis=0)
        pm = _dot_nt(qa4, bk)
        xall = jnp.concatenate([pm[0:c], pltpu.roll(pm[c:2 * c], c, 1)], axis=0)
        lbds.append(jnp.where(same_blk & strict_2c, xall, 0.0))
        laks.append(jnp.where((~same_blk) & strict_2c, xall, 0.0))
        prms.append(jnp.where(incl_2c, pm[2 * c:], 0.0))
        qa2s.append(jnp.concatenate([at0, at1], axis=0))
        vstk.append(jnp.concatenate([jnp.where(m0, 0.0, vc), jnp.where(m0, vc, 0.0)], axis=0))
        gams.append(gam)
        bkbars.append(jnp.concatenate([bc * gam, kc * gam], axis=0))

    tinvs = _unit_lower_inverses(lbds, c)
    w1s = [_dot(lak, vs) for lak, vs in zip(laks, vstk)]
    tzs = [_dot(tinv, jnp.concatenate([qa2, w1], axis=1))
           for tinv, qa2, w1 in zip(tinvs, qa2s, w1s)]
    r3s, y2s, phis, gs = [], [], [], []
    for (sl, rs), tz, prm, bkbar, gam in zip(items, tzs, prms, bkbars, gams):
        a2v2 = tz[0:c] + tz[c:2 * c]
        rhs = jnp.concatenate([a2v2, jnp.concatenate([zeros_c, v[rs, sl]], axis=1)], axis=0)
        res = _dot(jnp.concatenate([prm, bkbar.T], axis=0), rhs)
        ry = jnp.where(m0w, res[0:c], res[c:2 * c])
        r3s.append(rt[rs, sl] + ry[:, 0:LANES])
        y2s.append(ry[:, LANES:])
        pg = res[2 * c:]
        phis.append(diag_eye * gam + jnp.where(bd_mask, pg[:, 0:LANES], 0.0))
        gs.append(jnp.where(bd_mask, pg[:, LANES:], 0.0))

    hs = [h_ref[i] for i in range(n_pairs)]
    ys = [[] for _ in range(n_pairs)]
    for j in range(n_chunks):
        for i in range(n_pairs):
            idx = i * n_chunks + j
            ys[i].append(_dot(r3s[idx], hs[i]) + y2s[idx])
        hs = [_dot(phis[i * n_chunks + j], hs[i]) + gs[i * n_chunks + j] for i in range(n_pairs)]
    for i, sl in enumerate(pairs):
        h_ref[i] = hs[i]
        y = jnp.concatenate(ys[i], axis=0)
        mean = _dot_xr(y, ones_head, 2) * (1.0 / HEAD_DIM)
        yc_ = y - mean
        var = _dot_xr(yc_ * yc_, ones_head, 2) * (1.0 / HEAD_DIM)
        yn = yc_ * lax.rsqrt(var + GN_EPS) * lnw_ref[:, sl] + lnb_ref[:, sl]
        bonus = _dot_xr(r[:, sl] * k2[:, sl] * rk_ref[:, sl], ones_head, 2) * v[:, sl]
        o_ref[0, :, sl] = (yn + bonus).astype(o_ref.dtype)


def _rwkv(pa, mu, wl, w0, a0, k_k, k_a, r_k, ln_w, ln_b, tb=256, c=64):
    b, s, width = pa.shape
    da = w0.shape[-1]
    vec = lambda n: pl.BlockSpec((1, n), lambda bi, ti: (0, 0))
    return pl.pallas_call(
        functools.partial(_rwkv_kernel, tb=tb, c=c, da=da),
        grid=(b, s // tb),
        in_specs=[pl.BlockSpec((1, tb, width), lambda bi, ti: (bi, ti, 0)),
                  vec(width),
                  pl.BlockSpec(wl.shape, lambda bi, ti: (0, 0)),
                  vec(da), vec(da), vec(da), vec(da), vec(da), vec(da), vec(da)],
        out_specs=pl.BlockSpec((1, tb, da), lambda bi, ti: (bi, ti, 0)),
        out_shape=jax.ShapeDtypeStruct((b, s, da), BF16),
        scratch_shapes=[pltpu.VMEM((1, width), F32),
                        pltpu.VMEM((da // LANES, LANES, LANES), F32)],
        compiler_params=_cparams(("parallel", "arbitrary")),
        name="rwkv7",
    )(pa, mu, wl, w0, a0, k_k, k_a, r_k, ln_w, ln_b)


def kernel(x, norm_w, w_in, b_f, mu, w0, w_up, a0, a_up, k_k, k_a, r_k, ln_x_w, ln_x_b, w_out,
           final_norm_w):
    b, s, d = x.shape
    depth = w_in.shape[0]
    da = w0.shape[-1]
    lora = w_up.shape[1]
    hc = b_f.shape[-1]
    d_mix = w_out.shape[1]
    db = (d_mix - da) // 2
    n_shift = 3 * da + 2 * lora
    assert da % LANES == 0 and 2 * lora == LANES and db == hc * HEAD_DIM and hc % 2 == 1
    assert mu.shape[-1] == n_shift
    n_pairs_attn = (db + HEAD_DIM) // LANES
    dbp = n_pairs_attn * LANES
    scale = HEAD_DIM ** -0.5

    x2 = x.reshape(b * s, d)
    zpad = lambda rows, n: jnp.zeros((rows, n), F32)
    for l in range(depth):
        wl_in = w_in[l]
        o = n_shift
        qb, kb, vb = (wl_in[:, o + i * db:o + (i + 1) * db] for i in range(3))
        o += 3 * db
        qc, kc, vc = (wl_in[:, o + i * db:o + (i + 1) * db] for i in range(3))
        o += 3 * db
        wf = wl_in[:, o:o + hc]
        wg = wl_in[:, o + hc:]
        w_all = jnp.concatenate(
            [wl_in[:, :n_shift],
             qb * scale, qc * scale, kb, kc, vb, vc,
             zpad(d, 1), wf, zpad(d, LANES - 1 - hc),
             wg[:, :da + db], zpad(d, 2 * HEAD_DIM), wg[:, da + db:]], axis=1).astype(BF16)
        widths = (n_shift, 2 * db, 2 * db, 2 * db, LANES, da + 2 * dbp)
        dtypes = (F32, BF16, BF16, BF16, F32, BF16)
        pa, q, k, v, f, g = _inproj(x2, norm_w[l][None, :], w_all, widths, dtypes)

        wl_comb = jnp.concatenate(
            [jnp.concatenate([w_up[l], zpad(lora, da)], axis=1),
             jnp.concatenate([zpad(lora, da), a_up[l]], axis=1)], axis=0).astype(BF16)
        row = lambda t: t.reshape(1, -1)
        ya = _rwkv(pa.reshape(b, s, n_shift), row(mu[l]), wl_comb, row(w0[l]), row(a0[l]),
                   row(k_k[l]), row(k_a[l]), row(r_k[l]), row(ln_x_w[l]), row(ln_x_b[l]))

        q3, k3, v3 = (t.reshape(b, s, 2 * db) for t in (q, k, v))
        sb_heads = tuple((g // 2, g % 2) for g in range(hc))
        yb = _sb_attention(q3, k3, v3, sb_heads, n_pairs_attn)
        bf_pad = jnp.concatenate([jnp.zeros((1,), F32), b_f[l], jnp.zeros((LANES - 1 - hc,), F32)])
        cum_f = _fcum(f.reshape(b, s, LANES), bf_pad[None, :])
        fox_heads = tuple(((hc + j) // 2, (hc + j) % 2, j + 1) for j in range(hc))
        yc = _fox_attention(q3, k3, v3, cum_f, fox_heads, first_blk=n_pairs_attn - 1,
                            n_blocks=n_pairs_attn)

        wo = w_out[l]
        wo_p = jnp.concatenate([wo[:da + db], zpad(2 * HEAD_DIM, d), wo[da + db:]], axis=0).astype(BF16)
        x2 = _outproj(x2, ya.reshape(b * s, da), yb.reshape(b * s, dbp), yc.reshape(b * s, dbp),
                      g, wo_p, final_norm_w[None, :], final=(l == depth - 1))
    return x2.reshape(b, s, d)
```

```python
import functools

import jax
import jax.numpy as jnp
from jax import lax
from jax.experimental import pallas as pl
from jax.experimental.pallas import tpu as pltpu

HEAD_DIM = 64
LANES = 128
NORM_EPS = 1e-6
GN_EPS = 64e-5
VMEM_LIMIT = 56 * 1024 * 1024
EXP_ZERO = -104.0
SHIFT_SAFE = 80.0

F32 = jnp.float32
BF16 = jnp.bfloat16


def _iota(shape, dim):
    return lax.broadcasted_iota(jnp.int32, shape, dim)


def _blk(idx, size):
    assert size & (size - 1) == 0
    return jnp.right_shift(idx, size.bit_length() - 1)


def _rem(idx, size):
    assert size & (size - 1) == 0
    return jnp.bitwise_and(idx, size - 1)


def _dot(a, b):
    return jnp.dot(a.astype(BF16), b.astype(BF16), preferred_element_type=F32)


def _dot_nt(a, b):
    return lax.dot_general(a.astype(BF16), b.astype(BF16), (((1,), (1,)), ((), ())),
                           preferred_element_type=F32)


def _split(x, n):
    parts = []
    rem = x
    for i in range(n):
        p = rem.astype(BF16)
        parts.append(p)
        if i + 1 < n:
            rem = rem - p.astype(F32)
    return parts


def _dot_xr(x, r, n):
    out = None
    for p in _split(x, n):
        t = jnp.dot(p, r, preferred_element_type=F32)
        out = t if out is None else out + t
    return out


def _dot_lx(l, x, n):
    out = None
    for p in _split(x, n):
        t = jnp.dot(l, p, preferred_element_type=F32)
        out = t if out is None else out + t
    return out


def _log_sigmoid(z):
    return jnp.minimum(z, 0.0) - jnp.log1p(jnp.exp(-jnp.abs(z)))


def _sigmoid(z):
    return 1.0 / (1.0 + jnp.exp(-z))


def _head_lanes(lane, h):
    return (lane >= h * HEAD_DIM) & (lane < (h + 1) * HEAD_DIM)


def _cparams(sem):
    return pltpu.CompilerParams(dimension_semantics=sem, vmem_limit_bytes=VMEM_LIMIT)


def _inproj_kernel(x_ref, nw_ref, w_ref, pa_ref, q_ref, k_ref, v_ref, f_ref, g_ref, *, n_chunk):
    x = x_ref[...]
    xn = x * lax.rsqrt(jnp.mean(x * x, axis=-1, keepdims=True) + NORM_EPS)
    h = (xn * nw_ref[...]).astype(BF16)
    off = 0
    for ref in (pa_ref, q_ref, k_ref, v_ref, f_ref, g_ref):
        width = ref.shape[-1]
        for c0 in range(0, width, n_chunk):
            c1 = min(c0 + n_chunk, width)
            ref[:, c0:c1] = jnp.dot(h, w_ref[:, off + c0:off + c1],
                                    preferred_element_type=F32).astype(ref.dtype)
        off += width


def _inproj(x2, nw, w, widths, dtypes, tm=512, n_chunk=640):
    m, d = x2.shape
    n = w.shape[1]
    assert sum(widths) == n and m % tm == 0
    out_shape = [jax.ShapeDtypeStruct((m, wd), dt) for wd, dt in zip(widths, dtypes)]
    out_specs = [pl.BlockSpec((tm, wd), lambda i: (i, 0)) for wd in widths]
    return pl.pallas_call(
        functools.partial(_inproj_kernel, n_chunk=n_chunk),
        grid=(m // tm,),
        in_specs=[pl.BlockSpec((tm, d), lambda i: (i, 0)),
                  pl.BlockSpec((1, d), lambda i: (0, 0)),
                  pl.BlockSpec((d, n), lambda i: (0, 0))],
        out_specs=out_specs,
        out_shape=out_shape,
        compiler_params=_cparams(("parallel",)),
        name="inproj",
    )(x2, nw, w)


def _outproj_kernel(x_ref, ya_ref, yb_ref, yc_ref, g_ref, w_ref, fw_ref, o_ref, *, final):
    da = ya_ref.shape[-1]
    db = yb_ref.shape[-1]
    acc = x_ref[...]
    off = 0
    for y_ref, wd in ((ya_ref, da), (yb_ref, db), (yc_ref, db)):
        g = g_ref[:, off:off + wd].astype(F32)
        yg = y_ref[...].astype(F32) * (g * _sigmoid(g))
        acc = acc + jnp.dot(yg.astype(BF16), w_ref[off:off + wd, :], preferred_element_type=F32)
        off += wd
    if final:
        acc = acc * lax.rsqrt(jnp.mean(acc * acc, axis=-1, keepdims=True) + NORM_EPS) * fw_ref[...]
    o_ref[...] = acc


def _outproj(x2, ya, yb, yc, g, w, fw, final, tm=512):
    m, d = x2.shape
    row = lambda wd: pl.BlockSpec((tm, wd), lambda i: (i, 0))
    return pl.pallas_call(
        functools.partial(_outproj_kernel, final=final),
        grid=(m // tm,),
        in_specs=[row(d), row(ya.shape[1]), row(yb.shape[1]), row(yc.shape[1]), row(g.shape[1]),
                  pl.BlockSpec(w.shape, lambda i: (0, 0)),
                  pl.BlockSpec((1, d), lambda i: (0, 0))],
        out_specs=row(d),
        out_shape=jax.ShapeDtypeStruct((m, d), F32),
        compiler_params=_cparams(("parallel",)),
        name="outproj",
    )(x2, ya, yb, yc, g, w, fw)


def _fcum_kernel(f_ref, bf_ref, c_ref, *, blk):
    s = f_ref.shape[1]
    row = _iota((blk, blk), 0)
    col = _iota((blk, blk), 1)
    tri = (col <= row).astype(BF16)
    carry = jnp.zeros((1, LANES), F32)
    for i in range(s // blk):
        lf = _log_sigmoid(f_ref[0, i * blk:(i + 1) * blk, :] + bf_ref[...])
        c = _dot_lx(tri, lf, 3) + carry
        c_ref[0, i * blk:(i + 1) * blk, :] = c
        carry = c[blk - 1:blk, :]


def _fcum(f, bf, blk=256):
    b, s, _ = f.shape
    return pl.pallas_call(
        functools.partial(_fcum_kernel, blk=blk),
        grid=(b,),
        in_specs=[pl.BlockSpec((1, s, LANES), lambda i: (i, 0, 0)),
                  pl.BlockSpec((1, LANES), lambda i: (0, 0))],
        out_specs=pl.BlockSpec((1, s, LANES), lambda i: (i, 0, 0)),
        out_shape=jax.ShapeDtypeStruct((b, s, LANES), F32),
        compiler_params=_cparams(("parallel",)),
        name="fcum",
    )(f, bf)


def _sb_kernel(q_ref, k_ref, v_ref, o_ref, *, t, heads):
    qi = pl.program_id(1)
    lane = _iota((t, LANES), 1)
    row = _iota((t, t), 0)
    col = _iota((t, t), 1)
    strict = col < row
    tri = (row > col).astype(BF16)
    q0 = pl.multiple_of(qi * t, t)
    p0 = pl.multiple_of(jnp.maximum(qi - 1, 0) * t, t)
    no_prev = jnp.where(qi > 0, 0.0, -1e30).astype(F32)

    def kv(k0, blk):
        sl = slice(blk * LANES, (blk + 1) * LANES)
        return k_ref[0, pl.ds(k0, t), sl], v_ref[0, pl.ds(k0, t), sl]

    def logits(qh, k2, diag):
        z = _dot_nt(qh, k2)
        ls = jnp.minimum(z, 0.0) - jnp.log(1.0 + jnp.exp(-jnp.abs(z)))
        log1m = ls - z
        if diag:
            log1m = jnp.where(strict, log1m, 0.0)
        return ls, _dot_xr(log1m, tri, 1), jnp.sum(log1m, axis=-1, keepdims=True)

    qhs, accs, carries = [], [], []
    for blk, half in heads:
        q2 = q_ref[0, :, blk * LANES:(blk + 1) * LANES]
        qh = jnp.where(_head_lanes(lane, half), q2, jnp.zeros_like(q2))
        kd, vd = kv(q0, blk)
        kp, vp = kv(p0, blk)
        ls_d, after_d, mass_d = logits(qh, kd, True)
        ls_p, after_p, mass_p = logits(qh, kp, False)
        attn_d = jnp.where(strict, jnp.exp(ls_d + after_d), 0.0)
        attn_p = jnp.exp(ls_p + after_p + (mass_d + no_prev))
        qhs.append(qh)
        accs.append(_dot(attn_d, vd) + _dot(attn_p, vp))
        carries.append(mass_d + mass_p)

    def any_live(carries):
        m = carries[0]
        for c in carries[1:]:
            m = jnp.maximum(m, c)
        return (jnp.max(m) > EXP_ZERO).astype(jnp.int32)

    def cond(st):
        return (st[0] >= 0) & (st[1] > 0)

    def body(st):
        kb, _, accs, carries = st
        k0 = pl.multiple_of(kb * t, t)
        new_accs, new_carries = [], []
        for (blk, _), qh, acc, carry in zip(heads, qhs, accs, carries):
            k2, v2 = kv(k0, blk)
            ls, after, mass = logits(qh, k2, False)
            new_accs.append(acc + _dot(jnp.exp(ls + after + carry), v2))
            new_carries.append(carry + mass)
        return kb - 1, any_live(new_carries), tuple(new_accs), tuple(new_carries)

    st = lax.while_loop(cond, body, (qi - 2, any_live(carries), tuple(accs), tuple(carries)))
    accs = st[2]
    for blk in sorted({b for b, _ in heads}):
        halves = {h: a for (b, h), a in zip(heads, accs) if b == blk}
        zero = jnp.zeros((t, LANES), F32)
        o_ref[0, :, blk * LANES:(blk + 1) * LANES] = jnp.where(
            lane < HEAD_DIM, halves.get(0, zero), halves.get(1, zero)).astype(o_ref.dtype)


def _sb_attention(q, k, v, heads, n_blocks, t=256):
    b, s, w = q.shape
    return pl.pallas_call(
        functools.partial(_sb_kernel, t=t, heads=heads),
        grid=(b, s // t),
        in_specs=[pl.BlockSpec((1, t, w), lambda bi, qi: (bi, qi, 0)),
                  pl.BlockSpec((1, s, w), lambda bi, qi: (bi, 0, 0)),
                  pl.BlockSpec((1, s, w), lambda bi, qi: (bi, 0, 0))],
        out_specs=pl.BlockSpec((1, t, n_blocks * LANES), lambda bi, qi: (bi, qi, 0)),
        out_shape=jax.ShapeDtypeStruct((b, s, n_blocks * LANES), BF16),
        compiler_params=_cparams(("parallel", "arbitrary")),
        name="stickbreak",
    )(q, k, v)


def _bias_lanes(lane, base, own, vals, ones_first):
    slots = ([None] * 3 + list(vals)) if ones_first else (list(vals) + [None] * 3)
    out = own
    for i, val in enumerate(slots):
        fill = jnp.ones((), BF16) if val is None else val
        out = jnp.where(lane == base + i, fill, out)
    return out


def _fox_kernel(q_ref, k_ref, v_ref, cq_ref, ck_ref, o_ref, kaug_ref, vaug_ref, acc_ref, kmax_ref,
                *, t, heads, first_blk):
    qi = pl.program_id(1)
    s_len = k_ref.shape[1]
    lane = _iota((t, LANES), 1)
    causal = _iota((t, t), 1) <= _iota((t, t), 0)
    q0 = pl.multiple_of(qi * t, t)
    spare = [HEAD_DIM * (1 - half) for _, half, _ in heads]

    @pl.when(qi == 0)
    def _():
        lane_s = _iota((s_len, LANES), 1)
        call = ck_ref[0]
        for i, (blk, half, ci) in enumerate(heads):
            k2 = k_ref[0, :, blk * LANES:(blk + 1) * LANES]
            kf = jnp.where(_head_lanes(lane_s, half), k2, jnp.zeros_like(k2)).astype(F32)
            kmax_ref[i] = jnp.max(jnp.sqrt(jnp.sum(kf * kf, axis=-1, keepdims=True)))
            ck = jnp.sum(jnp.where(lane_s == ci, call, 0.0), axis=-1, keepdims=True)
            kaug_ref[i] = _bias_lanes(lane_s, spare[i], k2, [-p for p in _split(ck, 3)],
                                      ones_first=True)
            v2 = v_ref[0, :, blk * LANES:(blk + 1) * LANES]
            vaug_ref[i] = jnp.where(lane_s == spare[i], jnp.ones((), BF16), v2)

    qas, ubound = [], None
    for i, (blk, half, ci) in enumerate(heads):
        q2 = q_ref[0, :, blk * LANES:(blk + 1) * LANES]
        own = jnp.where(_head_lanes(lane, half), q2, jnp.zeros_like(q2))
        qf = own.astype(F32)
        u = jnp.sqrt(jnp.sum(qf * qf, axis=-1, keepdims=True)) * kmax_ref[i]
        ubound = u if ubound is None else jnp.maximum(ubound, u)
        cq = jnp.sum(jnp.where(lane == ci, cq_ref[0], 0.0), axis=-1, keepdims=True)
        qas.append(_bias_lanes(lane, spare[i], own, _split(cq - u, 3), ones_first=False))
    safe = 2.0 * jnp.max(ubound) < SHIFT_SAFE

    def logits(i, k0, diag):
        s_blk = _dot_nt(qas[i], kaug_ref[i, pl.ds(k0, t), :])
        return jnp.where(causal, s_blk, -1e30) if diag else s_blk

    def write(outs):
        for blk in sorted({b for b, _, _ in heads}):
            halves = {h: o for (b, h, _), o in zip(heads, outs) if b == blk}
            zero = jnp.zeros((t, LANES), F32)
            ob = blk - first_blk
            o_ref[0, :, ob * LANES:(ob + 1) * LANES] = jnp.where(
                lane < HEAD_DIM, halves.get(0, zero), halves.get(1, zero)).astype(o_ref.dtype)

    @pl.when(safe)
    def _():
        def sweep(k0, diag):
            for i in range(len(heads)):
                p = jnp.exp(logits(i, k0, diag))
                acc_ref[i] += _dot(p, vaug_ref[i, pl.ds(k0, t), :])

        acc_ref[...] = jnp.zeros_like(acc_ref)

        @pl.loop(0, qi)
        def _(j):
            sweep(pl.multiple_of(j * t, t), False)

        sweep(q0, True)
        outs = []
        for i in range(len(heads)):
            acc = acc_ref[i]
            l = jnp.sum(jnp.where(lane == spare[i], acc, 0.0), axis=-1, keepdims=True)
            outs.append(acc * (1.0 / l))
        write(outs)

    @pl.when(jnp.logical_not(safe))
    def _():
        def update(st, s_blk, v2):
            m, l, acc = st
            m_new = jnp.maximum(m, jnp.max(s_blk, axis=-1, keepdims=True))
            alpha = jnp.exp(m - m_new)
            p = jnp.exp(s_blk - m_new)
            l = alpha * l + jnp.sum(p, axis=-1, keepdims=True)
            acc = alpha * acc + _dot(p, v2)
            return m_new, l, acc

        def sweep(k0, sts, diag):
            return tuple(update(st, logits(i, k0, diag), vaug_ref[i, pl.ds(k0, t), :])
                         for i, st in enumerate(sts))

        init = tuple((jnp.full((t, 1), -1e30, F32), jnp.zeros((t, 1), F32),
                      jnp.zeros((t, LANES), F32)) for _ in heads)
        sts = lax.fori_loop(0, qi, lambda j, sts: sweep(pl.multiple_of(j * t, t), sts, False),
                            init)
        sts = sweep(q0, sts, True)
        write([acc * (1.0 / l) for _, l, acc in sts])


def _fox_attention(q, k, v, c, heads, first_blk, n_blocks, t=512):
    b, s, w = q.shape
    return pl.pallas_call(
        functools.partial(_fox_kernel, t=t, heads=heads, first_blk=first_blk),
        grid=(b, s // t),
        in_specs=[pl.BlockSpec((1, t, w), lambda bi, qi: (bi, qi, 0)),
                  pl.BlockSpec((1, s, w), lambda bi, qi: (bi, 0, 0)),
                  pl.BlockSpec((1, s, w), lambda bi, qi: (bi, 0, 0)),
                  pl.BlockSpec((1, t, LANES), lambda bi, qi: (bi, qi, 0)),
                  pl.BlockSpec((1, s, LANES), lambda bi, qi: (bi, 0, 0))],
        out_specs=pl.BlockSpec((1, t, n_blocks * LANES), lambda bi, qi: (bi, qi, 0)),
        out_shape=jax.ShapeDtypeStruct((b, s, n_blocks * LANES), BF16),
        scratch_shapes=[pltpu.VMEM((len(heads), s, LANES), BF16),
                        pltpu.VMEM((len(heads), s, LANES), BF16),
                        pltpu.VMEM((len(heads), t, LANES), F32),
                        pltpu.SMEM((len(heads),), F32)],
        compiler_params=_cparams(("parallel", "arbitrary")),
        name="fox",
    )(q, k, v, c, c)


def _unit_lower_inverses(lbds, c):
    n = lbds[0].shape[0]
    row = _iota((n, n), 0)
    col = _iota((n, n), 1)
    eye = (row == col).astype(F32)
    same2 = _blk(row, 2) == _blk(col, 2)
    xs = [eye + jnp.where(same2, l, 0.0) for l in lbds]
    s = 2
    while s < c:
        join = (_blk(row, 2 * s) == _blk(col, 2 * s)) & (_blk(row, s) != _blk(col, s))
        ts = [_dot(jnp.where(join, l, 0.0), x) for l, x in zip(lbds, xs)]
        xs = [x + _dot(x, t) for x, t in zip(xs, ts)]
        s *= 2
    return xs


def _rwkv_kernel(pa_ref, mu_ref, wl_ref, w0_ref, a0_ref, kk_ref, ka_ref, rk_ref, lnw_ref, lnb_ref,
                 o_ref, prev_ref, h_ref, *, tb, c, da):
    ti = pl.program_id(1)
    n_pairs = da // LANES
    n_chunks = tb // c
    width = pa_ref.shape[-1]
    assert 2 * c == LANES

    @pl.when(ti == 0)
    def _():
        prev_ref[...] = jnp.zeros_like(prev_ref)
        h_ref[...] = jnp.zeros_like(h_ref)

    p = pa_ref[0]
    prow = _iota((tb, width), 0)
    prev = jnp.where(prow == 0, prev_ref[...], pltpu.roll(p, 1, 0))
    prev_ref[...] = p[tb - 1:tb, :]
    xs = p + mu_ref[...] * (prev - p)
    r = xs[:, 0:da]
    k = xs[:, da:2 * da]
    v = xs[:, 2 * da:3 * da]
    wa = xs[:, 3 * da:3 * da + LANES]
    lane_t = _iota((tb, LANES), 1)
    wa = jnp.where(lane_t < HEAD_DIM, jnp.tanh(wa), wa)
    lora = _dot(wa, wl_ref[...])
    wpre = w0_ref[...] + lora[:, 0:da]
    w = -(jnp.maximum(-wpre, 0.0) + jnp.log1p(jnp.exp(-jnp.abs(wpre)))) - 0.5
    lw = -jnp.exp(w)
    a = _sigmoid(a0_ref[...] + lora[:, da:2 * da])
    k2 = k * (1.0 + (a - 1.0) * ka_ref[...])
    kk = k * kk_ref[...]

    er = _blk(_iota((LANES, LANES), 0), HEAD_DIM)
    ec = _blk(_iota((LANES, LANES), 1), HEAD_DIM)
    bd_mask = er == ec
    ones_head = bd_mask.astype(BF16)
    diag_eye = (_iota((LANES, LANES), 0) == _iota((LANES, LANES), 1)).astype(F32)

    pairs = [slice(i * LANES, (i + 1) * LANES) for i in range(n_pairs)]
    kkn = jnp.concatenate(
        [kk[:, sl] * lax.rsqrt(jnp.maximum(_dot_xr(kk[:, sl] * kk[:, sl], ones_head, 2), 1e-24))
         for sl in pairs], axis=1)
    av = -kkn
    bv = kkn * a

    grp = min(tb, 2 * LANES)
    tr = _iota((grp, grp), 0)
    tc = _iota((grp, grp), 1)
    tri_blk = ((_blk(tr, c) == _blk(tc, c)) & (tc <= tr)).astype(BF16)
    cum = jnp.concatenate([_dot_lx(tri_blk, lw[i:i + grp], 2) for i in range(0, tb, grp)], axis=0)
    eg = jnp.exp(cum)
    rt = r * eg
    at = av * jnp.exp(cum - lw)
    ieg = jnp.exp(-cum)
    kt = k2 * ieg
    bt = bv * ieg

    rr = _iota((2 * c, 2 * c), 0)
    cc = _iota((2 * c, 2 * c), 1)
    same_blk = _blk(rr, c) == _blk(cc, c)
    strict_2c = _rem(cc, c) < _rem(rr, c)
    incl_2c = _rem(cc, c) <= _rem(rr, c)
    m0 = _iota((c, LANES), 1) < HEAD_DIM
    m0w = _rem(_iota((c, 2 * LANES), 1), LANES) < HEAD_DIM
    zeros_c = jnp.zeros((c, LANES), F32)

    items = [(sl, slice(j * c, (j + 1) * c)) for sl in pairs for j in range(n_chunks)]
    lbds, laks, prms, qa2s, vstk, gams, bkbars = [], [], [], [], [], [], []
    for sl, rs in items:
        rc, ac, kc, bc, vc = rt[rs, sl], at[rs, sl], kt[rs, sl], bt[rs, sl], v[rs, sl]
        gam = eg[rs.stop - 1:rs.stop, sl]
        bk = jnp.concatenate([bc, kc], axis=0)
        at0 = jnp.where(m0, ac, 0.0)
        at1 = jnp.where(m0, 0.0, ac)
        qa4 = jnp.concatenate([at0, at1, jnp.where(m0, rc, 0.0), jnp.where(m0, 0.0, rc)], axis=0)
        pm = _dot_nt(qa4, bk)
        xall = jnp.concatenate([pm[0:c], pltpu.roll(pm[c:2 * c], c, 1)], axis=0)
        lbds.append(jnp.where(same_blk & strict_2c, xall, 0.0))
        laks.append(jnp.where((~same_blk) & strict_2c, xall, 0.0))
        prms.append(jnp.where(incl_2c, pm[2 * c:], 0.0))
        qa2s.append(jnp.concatenate([at0, at1], axis=0))
        vstk.append(jnp.concatenate([jnp.where(m0, 0.0, vc), jnp.where(m0, vc, 0.0)], axis=0))
        gams.append(gam)
        bkbars.append(jnp.concatenate([bc * gam, kc * gam], axis=0))

    tinvs = _unit_lower_inverses(lbds, c)
    w1s = [_dot(lak, vs) for lak, vs in zip(laks, vstk)]
    tzs = [_dot(tinv, jnp.concatenate([qa2, w1], axis=1))
           for tinv, qa2, w1 in zip(tinvs, qa2s, w1s)]
    r3s, y2s, phis, gs = [], [], [], []
    for (sl, rs), tz, prm, bkbar, gam in zip(items, tzs, prms, bkbars, gams):
        a2v2 = tz[0:c] + tz[c:2 * c]
        rhs = jnp.concatenate([a2v2, jnp.concatenate([zeros_c, v[rs, sl]], axis=1)], axis=0)
        res = _dot(jnp.concatenate([prm, bkbar.T], axis=0), rhs)
        ry = jnp.where(m0w, res[0:c], res[c:2 * c])
        r3s.append(rt[rs, sl] + ry[:, 0:LANES])
        y2s.append(ry[:, LANES:])
        pg = res[2 * c:]
        phis.append(diag_eye * gam + jnp.where(bd_mask, pg[:, 0:LANES], 0.0))
        gs.append(jnp.where(bd_mask, pg[:, LANES:], 0.0))

    hs = [h_ref[i] for i in range(n_pairs)]
    ys = [[] for _ in range(n_pairs)]
    for j in range(n_chunks):
        for i in range(n_pairs):
            idx = i * n_chunks + j
            ys[i].append(_dot(r3s[idx], hs[i]) + y2s[idx])
        hs = [_dot(phis[i * n_chunks + j], hs[i]) + gs[i * n_chunks + j] for i in range(n_pairs)]
    for i, sl in enumerate(pairs):
        h_ref[i] = hs[i]
        y = jnp.concatenate(ys[i], axis=0)
        mean = _dot_xr(y, ones_head, 2) * (1.0 / HEAD_DIM)
        yc_ = y - mean
        var = _dot_xr(yc_ * yc_, ones_head, 2) * (1.0 / HEAD_DIM)
        yn = yc_ * lax.rsqrt(var + GN_EPS) * lnw_ref[:, sl] + lnb_ref[:, sl]
        bonus = _dot_xr(r[:, sl] * k2[:, sl] * rk_ref[:, sl], ones_head, 2) * v[:, sl]
        o_ref[0, :, sl] = (yn + bonus).astype(o_ref.dtype)


def _rwkv(pa, mu, wl, w0, a0, k_k, k_a, r_k, ln_w, ln_b, tb=512, c=64):
    b, s, width = pa.shape
    da = w0.shape[-1]
    vec = lambda n: pl.BlockSpec((1, n), lambda bi, ti: (0, 0))
    return pl.pallas_call(
        functools.partial(_rwkv_kernel, tb=tb, c=c, da=da),
        grid=(b, s // tb),
        in_specs=[pl.BlockSpec((1, tb, width), lambda bi, ti: (bi, ti, 0)),
                  vec(width),
                  pl.BlockSpec(wl.shape, lambda bi, ti: (0, 0)),
                  vec(da), vec(da), vec(da), vec(da), vec(da), vec(da), vec(da)],
        out_specs=pl.BlockSpec((1, tb, da), lambda bi, ti: (bi, ti, 0)),
        out_shape=jax.ShapeDtypeStruct((b, s, da), BF16),
        scratch_shapes=[pltpu.VMEM((1, width), F32),
                        pltpu.VMEM((da // LANES, LANES, LANES), F32)],
        compiler_params=_cparams(("parallel", "arbitrary")),
        name="rwkv7",
    )(pa, mu, wl, w0, a0, k_k, k_a, r_k, ln_w, ln_b)


def kernel(x, norm_w, w_in, b_f, mu, w0, w_up, a0, a_up, k_k, k_a, r_k, ln_x_w, ln_x_b, w_out,
           final_norm_w):
    b, s, d = x.shape
    depth = w_in.shape[0]
    da = w0.shape[-1]
    lora = w_up.shape[1]
    hc = b_f.shape[-1]
    d_mix = w_out.shape[1]
    db = (d_mix - da) // 2
    n_shift = 3 * da + 2 * lora
    assert da % LANES == 0 and 2 * lora == LANES and db == hc * HEAD_DIM and hc % 2 == 1
    assert mu.shape[-1] == n_shift
    n_pairs_attn = (db + HEAD_DIM) // LANES
    dbp = n_pairs_attn * LANES
    scale = HEAD_DIM ** -0.5

    x2 = x.reshape(b * s, d)
    zpad = lambda rows, n: jnp.zeros((rows, n), F32)
    for l in range(depth):
        wl_in = w_in[l]
        o = n_shift
        qb, kb, vb = (wl_in[:, o + i * db:o + (i + 1) * db] for i in range(3))
        o += 3 * db
        qc, kc, vc = (wl_in[:, o + i * db:o + (i + 1) * db] for i in range(3))
        o += 3 * db
        wf = wl_in[:, o:o + hc]
        wg = wl_in[:, o + hc:]
        w_all = jnp.concatenate(
            [wl_in[:, :n_shift],
             qb * scale, qc * scale, kb, kc, vb, vc,
             zpad(d, 1), wf, zpad(d, LANES - 1 - hc),
             wg[:, :da + db], zpad(d, 2 * HEAD_DIM), wg[:, da + db:]], axis=1).astype(BF16)
        widths = (n_shift, 2 * db, 2 * db, 2 * db, LANES, da + 2 * dbp)
        dtypes = (F32, BF16, BF16, BF16, F32, BF16)
        pa, q, k, v, f, g = _inproj(x2, norm_w[l][None, :], w_all, widths, dtypes)

        wl_comb = jnp.concatenate(
            [jnp.concatenate([w_up[l], zpad(lora, da)], axis=1),
             jnp.concatenate([zpad(lora, da), a_up[l]], axis=1)], axis=0).astype(BF16)
        row = lambda t: t.reshape(1, -1)
        ya = _rwkv(pa.reshape(b, s, n_shift), row(mu[l]), wl_comb, row(w0[l]), row(a0[l]),
                   row(k_k[l]), row(k_a[l]), row(r_k[l]), row(ln_x_w[l]), row(ln_x_b[l]))

        q3, k3, v3 = (t.reshape(b, s, 2 * db) for t in (q, k, v))
        sb_heads = tuple((g // 2, g % 2) for g in range(hc))
        yb = _sb_attention(q3, k3, v3, sb_heads, n_pairs_attn)
        bf_pad = jnp.concatenate([jnp.zeros((1,), F32), b_f[l], jnp.zeros((LANES - 1 - hc,), F32)])
        cum_f = _fcum(f.reshape(b, s, LANES), bf_pad[None, :])
        fox_heads = tuple(((hc + j) // 2, (hc + j) % 2, j + 1) for j in range(hc))
        yc = _fox_attention(q3, k3, v3, cum_f, fox_heads, first_blk=n_pairs_attn - 1,
                            n_blocks=n_pairs_attn)

        wo = w_out[l]
        wo_p = jnp.concatenate([wo[:da + db], zpad(2 * HEAD_DIM, d), wo[da + db:]], axis=0).astype(BF16)
        x2 = _outproj(x2, ya.reshape(b * s, da), yb.reshape(b * s, dbp), yc.reshape(b * s, dbp),
                      g, wo_p, final_norm_w[None, :], final=(l == depth - 1))
    return x2.reshape(b, s, d)
```

```python
import functools
from typing import NamedTuple

import jax
import jax.numpy as jnp
from jax import lax
from jax.experimental import pallas as pl
from jax.experimental.pallas import tpu as pltpu

HEAD_DIM = 64
LANES = 128
NORM_EPS = 1e-6
GN_EPS = 64e-5
VMEM_LIMIT = 56 * 1024 * 1024
EXP_ZERO = -104.0
SHIFT_SAFE = 80.0

F32 = jnp.float32
BF16 = jnp.bfloat16


class _Head(NamedTuple):
    q: tuple
    k: tuple
    v: tuple
    out: tuple
    c: int = -1


def _slot(col):
    assert col % HEAD_DIM == 0
    return (col // LANES, (col % LANES) // HEAD_DIM)


def _iota(shape, dim):
    return lax.broadcasted_iota(jnp.int32, shape, dim)


def _blk(idx, size):
    assert size & (size - 1) == 0
    return jnp.right_shift(idx, size.bit_length() - 1)


def _rem(idx, size):
    assert size & (size - 1) == 0
    return jnp.bitwise_and(idx, size - 1)


def _lanes(blk):
    return slice(blk * LANES, (blk + 1) * LANES)


def _dot(a, b):
    return jnp.dot(a.astype(BF16), b.astype(BF16), preferred_element_type=F32)


def _dot_nt(a, b):
    return lax.dot_general(a.astype(BF16), b.astype(BF16), (((1,), (1,)), ((), ())),
                           preferred_element_type=F32)


def _split(x, n):
    parts = []
    rem = x
    for i in range(n):
        p = rem.astype(BF16)
        parts.append(p)
        if i + 1 < n:
            rem = rem - p.astype(F32)
    return parts


def _dot_xr(x, r, n):
    out = None
    for p in _split(x, n):
        t = jnp.dot(p, r, preferred_element_type=F32)
        out = t if out is None else out + t
    return out


def _dot_lx(l, x, n):
    out = None
    for p in _split(x, n):
        t = jnp.dot(l, p, preferred_element_type=F32)
        out = t if out is None else out + t
    return out


def _log_sigmoid(z):
    return jnp.minimum(z, 0.0) - jnp.log1p(jnp.exp(-jnp.abs(z)))


def _sigmoid(z):
    return 1.0 / (1.0 + jnp.exp(-z))


def _head_lanes(lane, h):
    return (lane >= h * HEAD_DIM) & (lane < (h + 1) * HEAD_DIM)


def _query_in_key_half(q2, lane, hd):
    own = jnp.where(_head_lanes(lane, hd.q[1]), q2, jnp.zeros_like(q2))
    if hd.q[1] != hd.k[1]:
        own = pltpu.roll(own.astype(F32), HEAD_DIM, 1).astype(BF16)
    return own


def _write_heads(o_ref, lane, heads, outs):
    t = lane.shape[0]
    for blk in sorted({hd.out[0] for hd in heads}):
        halves = {}
        for hd, o in zip(heads, outs):
            assert hd.out[1] == hd.v[1]
            if hd.out[0] == blk:
                halves[hd.out[1]] = o
        zero = jnp.zeros((t, LANES), F32)
        o_ref[0, :, _lanes(blk)] = jnp.where(
            lane < HEAD_DIM, halves.get(0, zero), halves.get(1, zero)).astype(o_ref.dtype)


def _cparams(sem):
    return pltpu.CompilerParams(dimension_semantics=sem, vmem_limit_bytes=VMEM_LIMIT)


def _inproj_kernel(x_ref, nw_ref, w_ref, *out_refs, n_chunk):
    x = x_ref[...]
    xn = x * lax.rsqrt(jnp.mean(x * x, axis=-1, keepdims=True) + NORM_EPS)
    h = (xn * nw_ref[...]).astype(BF16)
    outs, off = [], 0
    for ref in out_refs:
        outs.append((ref, off, off + ref.shape[-1]))
        off += ref.shape[-1]
    for c0 in range(0, off, n_chunk):
        c1 = min(c0 + n_chunk, off)
        res = jnp.dot(h, w_ref[:, c0:c1], preferred_element_type=F32)
        for ref, o0, o1 in outs:
            a, b = max(c0, o0), min(c1, o1)
            if a < b:
                ref[:, a - o0:b - o0] = res[:, a - c0:b - c0].astype(ref.dtype)


def _inproj(x2, nw, w, widths, dtypes, tm=512, n_chunk=512):
    m, d = x2.shape
    n = w.shape[1]
    assert sum(widths) == n and m % tm == 0
    out_shape = [jax.ShapeDtypeStruct((m, wd), dt) for wd, dt in zip(widths, dtypes)]
    out_specs = [pl.BlockSpec((tm, wd), lambda i: (i, 0)) for wd in widths]
    return pl.pallas_call(
        functools.partial(_inproj_kernel, n_chunk=n_chunk),
        grid=(m // tm,),
        in_specs=[pl.BlockSpec((tm, d), lambda i: (i, 0)),
                  pl.BlockSpec((1, d), lambda i: (0, 0)),
                  pl.BlockSpec((d, n), lambda i: (0, 0), pipeline_mode=pl.Buffered(1))],
        out_specs=out_specs,
        out_shape=out_shape,
        compiler_params=_cparams(("parallel",)),
        name="inproj",
    )(x2, nw, w)


def _outproj_kernel(x_ref, ya_ref, yb_ref, yc_ref, g_ref, w_ref, fw_ref, o_ref, *, final):
    g = g_ref[...].astype(F32)
    y = jnp.concatenate([ya_ref[...], yb_ref[...], yc_ref[...]], axis=1).astype(F32)
    yg = (y * (g * _sigmoid(g))).astype(BF16)
    acc = x_ref[...] + jnp.dot(yg, w_ref[...], preferred_element_type=F32)
    if final:
        acc = acc * lax.rsqrt(jnp.mean(acc * acc, axis=-1, keepdims=True) + NORM_EPS) * fw_ref[...]
    o_ref[...] = acc


def _outproj(x2, ya, yb, yc, g, w, fw, final, tm=1024):
    m, d = x2.shape
    row = lambda wd: pl.BlockSpec((tm, wd), lambda i: (i, 0))
    return pl.pallas_call(
        functools.partial(_outproj_kernel, final=final),
        grid=(m // tm,),
        in_specs=[row(d), row(ya.shape[1]), row(yb.shape[1]), row(yc.shape[1]), row(g.shape[1]),
                  pl.BlockSpec(w.shape, lambda i: (0, 0)),
                  pl.BlockSpec((1, d), lambda i: (0, 0))],
        out_specs=row(d),
        out_shape=jax.ShapeDtypeStruct((m, d), F32),
        compiler_params=_cparams(("parallel",)),
        name="outproj",
    )(x2, ya, yb, yc, g, w, fw)


def _fcum_kernel(f_ref, bf_ref, c_ref, *, blk):
    s = f_ref.shape[1]
    row = _iota((blk, blk), 0)
    col = _iota((blk, blk), 1)
    tri = (col <= row).astype(BF16)
    carry = jnp.zeros((1, LANES), F32)
    for i in range(s // blk):
        lf = _log_sigmoid(f_ref[0, i * blk:(i + 1) * blk, :] + bf_ref[...])
        c = _dot_lx(tri, lf, 3) + carry
        c_ref[0, i * blk:(i + 1) * blk, :] = c
        carry = c[blk - 1:blk, :]


def _fcum(f, bf, blk=256):
    b, s, _ = f.shape
    return pl.pallas_call(
        functools.partial(_fcum_kernel, blk=blk),
        grid=(b,),
        in_specs=[pl.BlockSpec((1, s, LANES), lambda i: (i, 0, 0)),
                  pl.BlockSpec((1, LANES), lambda i: (0, 0))],
        out_specs=pl.BlockSpec((1, s, LANES), lambda i: (i, 0, 0)),
        out_shape=jax.ShapeDtypeStruct((b, s, LANES), F32),
        compiler_params=_cparams(("parallel",)),
        name="fcum",
    )(f, bf)


def _sb_kernel(q_ref, kv_ref, o_ref, *, t, heads):
    qi = pl.program_id(1)
    lane = _iota((t, LANES), 1)
    row = _iota((t, t), 0)
    col = _iota((t, t), 1)
    strict = col < row
    tri = (row > col).astype(BF16)
    q0 = pl.multiple_of(qi * t, t)
    p0 = pl.multiple_of(jnp.maximum(qi - 1, 0) * t, t)
    no_prev = jnp.where(qi > 0, 0.0, -1e30).astype(F32)

    def kv(k0, hd):
        return kv_ref[0, pl.ds(k0, t), _lanes(hd.k[0])], kv_ref[0, pl.ds(k0, t), _lanes(hd.v[0])]

    def logits(qh, k2, diag):
        z = _dot_nt(qh, k2)
        ls = jnp.minimum(z, 0.0) - jnp.log(1.0 + jnp.exp(-jnp.abs(z)))
        log1m = ls - z
        if diag:
            log1m = jnp.where(strict, log1m, 0.0)
        return ls, _dot_xr(log1m, tri, 1), jnp.sum(log1m, axis=-1, keepdims=True)

    qhs, accs, carries = [], [], []
    for hd in heads:
        qh = _query_in_key_half(q_ref[0, :, _lanes(hd.q[0])], lane, hd)
        kd, vd = kv(q0, hd)
        kp, vp = kv(p0, hd)
        ls_d, after_d, mass_d = logits(qh, kd, True)
        ls_p, after_p, mass_p = logits(qh, kp, False)
        attn_d = jnp.where(strict, jnp.exp(ls_d + after_d), 0.0)
        attn_p = jnp.exp(ls_p + after_p + (mass_d + no_prev))
        qhs.append(qh)
        accs.append(_dot(attn_d, vd) + _dot(attn_p, vp))
        carries.append(mass_d + mass_p)

    def any_live(carries):
        m = carries[0]
        for c in carries[1:]:
            m = jnp.maximum(m, c)
        return (jnp.max(m) > EXP_ZERO).astype(jnp.int32)

    def cond(st):
        return (st[0] >= 0) & (st[1] > 0)

    def body(st):
        kb, _, accs, carries = st
        k0 = pl.multiple_of(kb * t, t)
        new_accs, new_carries = [], []
        for hd, qh, acc, carry in zip(heads, qhs, accs, carries):
            k2, v2 = kv(k0, hd)
            ls, after, mass = logits(qh, k2, False)
            new_accs.append(acc + _dot(jnp.exp(ls + after + carry), v2))
            new_carries.append(carry + mass)
        return kb - 1, any_live(new_carries), tuple(new_accs), tuple(new_carries)

    st = lax.while_loop(cond, body, (qi - 2, any_live(carries), tuple(accs), tuple(carries)))
    _write_heads(o_ref, lane, heads, st[2])


def _sb_attention(qkv, heads, n_out_blocks, width, t=256):
    b, s, _ = qkv.shape
    return pl.pallas_call(
        functools.partial(_sb_kernel, t=t, heads=heads),
        grid=(b, s // t),
        in_specs=[pl.BlockSpec((1, t, width), lambda bi, qi: (bi, qi, 0)),
                  pl.BlockSpec((1, s, width), lambda bi, qi: (bi, 0, 0))],
        out_specs=pl.BlockSpec((1, t, n_out_blocks * LANES), lambda bi, qi: (bi, qi, 0)),
        out_shape=jax.ShapeDtypeStruct((b, s, n_out_blocks * LANES), BF16),
        compiler_params=_cparams(("parallel", "arbitrary")),
        name="stickbreak",
    )(qkv, qkv)


def _bias_lanes(lane, base, own, vals, ones_first):
    slots = ([None] * 3 + list(vals)) if ones_first else (list(vals) + [None] * 3)
    out = own
    for i, val in enumerate(slots):
        fill = jnp.ones((), BF16) if val is None else val
        out = jnp.where(lane == base + i, fill, out)
    return out


def _fox_kernel(q_ref, kv_ref, cq_ref, ck_ref, o_ref, kaug_ref, vaug_ref, acc_ref, kmax_ref,
                *, t, heads):
    qi = pl.program_id(1)
    s_len = kv_ref.shape[1]
    lane = _iota((t, LANES), 1)
    causal = _iota((t, t), 1) <= _iota((t, t), 0)
    q0 = pl.multiple_of(qi * t, t)
    bias_base = [HEAD_DIM * (1 - hd.k[1]) for hd in heads]
    ones_lane = [HEAD_DIM * (1 - hd.v[1]) for hd in heads]

    @pl.when(qi == 0)
    def _():
        lane_s = _iota((s_len, LANES), 1)
        call = ck_ref[0]
        for i, hd in enumerate(heads):
            k2 = kv_ref[0, :, _lanes(hd.k[0])]
            kf = jnp.where(_head_lanes(lane_s, hd.k[1]), k2, jnp.zeros_like(k2)).astype(F32)
            kmax_ref[i] = jnp.max(jnp.sqrt(jnp.sum(kf * kf, axis=-1, keepdims=True)))
            ck = jnp.sum(jnp.where(lane_s == hd.c, call, 0.0), axis=-1, keepdims=True)
            kaug_ref[i] = _bias_lanes(lane_s, bias_base[i], k2, [-p for p in _split(ck, 3)],
                                      ones_first=True)
            vaug_ref[i] = jnp.where(lane_s == ones_lane[i], jnp.ones((), BF16),
                                    kv_ref[0, :, _lanes(hd.v[0])])

    qas, ubound = [], None
    for i, hd in enumerate(heads):
        own = _query_in_key_half(q_ref[0, :, _lanes(hd.q[0])], lane, hd)
        qf = own.astype(F32)
        u = jnp.sqrt(jnp.sum(qf * qf, axis=-1, keepdims=True)) * kmax_ref[i]
        ubound = u if ubound is None else jnp.maximum(ubound, u)
        cq = jnp.sum(jnp.where(lane == hd.c, cq_ref[0], 0.0), axis=-1, keepdims=True)
        qas.append(_bias_lanes(lane, bias_base[i], own, _split(cq - u, 3), ones_first=False))
    safe = 2.0 * jnp.max(ubound) < SHIFT_SAFE

    def logits(i, k0, diag):
        s_blk = _dot_nt(qas[i], kaug_ref[i, pl.ds(k0, t), :])
        return jnp.where(causal, s_blk, -1e30) if diag else s_blk

    @pl.when(safe)
    def _():
        def sweep(k0, diag):
            for i in range(len(heads)):
                p = jnp.exp(logits(i, k0, diag))
                acc_ref[i] += _dot(p, vaug_ref[i, pl.ds(k0, t), :])

        acc_ref[...] = jnp.zeros_like(acc_ref)

        @pl.loop(0, qi)
        def _(j):
            sweep(pl.multiple_of(j * t, t), False)

        sweep(q0, True)
        outs = []
        for i in range(len(heads)):
            acc = acc_ref[i]
            l = jnp.sum(jnp.where(lane == ones_lane[i], acc, 0.0), axis=-1, keepdims=True)
            outs.append(acc * (1.0 / l))
        _write_heads(o_ref, lane, heads, outs)

    @pl.when(jnp.logical_not(safe))
    def _():
        def update(st, s_blk, v2):
            m, l, acc = st
            m_new = jnp.maximum(m, jnp.max(s_blk, axis=-1, keepdims=True))
            alpha = jnp.exp(m - m_new)
            p = jnp.exp(s_blk - m_new)
            l = alpha * l + jnp.sum(p, axis=-1, keepdims=True)
            acc = alpha * acc + _dot(p, v2)
            return m_new, l, acc

        def sweep(k0, sts, diag):
            return tuple(update(st, logits(i, k0, diag), vaug_ref[i, pl.ds(k0, t), :])
                         for i, st in enumerate(sts))

        init = tuple((jnp.full((t, 1), -1e30, F32), jnp.zeros((t, 1), F32),
                      jnp.zeros((t, LANES), F32)) for _ in heads)
        sts = lax.fori_loop(0, qi, lambda j, sts: sweep(pl.multiple_of(j * t, t), sts, False),
                            init)
        sts = sweep(q0, sts, True)
        _write_heads(o_ref, lane, heads, [acc * (1.0 / l) for _, l, acc in sts])


def _fox_attention(qkv, c, heads, n_out_blocks, t=512):
    b, s, w = qkv.shape
    return pl.pallas_call(
        functools.partial(_fox_kernel, t=t, heads=heads),
        grid=(b, s // t),
        in_specs=[pl.BlockSpec((1, t, w), lambda bi, qi: (bi, qi, 0)),
                  pl.BlockSpec((1, s, w), lambda bi, qi: (bi, 0, 0)),
                  pl.BlockSpec((1, t, LANES), lambda bi, qi: (bi, qi, 0)),
                  pl.BlockSpec((1, s, LANES), lambda bi, qi: (bi, 0, 0))],
        out_specs=pl.BlockSpec((1, t, n_out_blocks * LANES), lambda bi, qi: (bi, qi, 0)),
        out_shape=jax.ShapeDtypeStruct((b, s, n_out_blocks * LANES), BF16),
        scratch_shapes=[pltpu.VMEM((len(heads), s, LANES), BF16),
                        pltpu.VMEM((len(heads), s, LANES), BF16),
                        pltpu.VMEM((len(heads), t, LANES), F32),
                        pltpu.SMEM((len(heads),), F32)],
        compiler_params=_cparams(("parallel", "arbitrary")),
        name="fox",
    )(qkv, qkv, c, c)


def _unit_lower_inverses(lbds, c):
    n = lbds[0].shape[0]
    row = _iota((n, n), 0)
    col = _iota((n, n), 1)
    eye = (row == col).astype(F32)
    same2 = _blk(row, 2) == _blk(col, 2)
    xs = [eye + jnp.where(same2, l, 0.0) for l in lbds]
    s = 2
    while s < c:
        join = (_blk(row, 2 * s) == _blk(col, 2 * s)) & (_blk(row, s) != _blk(col, s))
        ts = [_dot(jnp.where(join, l, 0.0), x) for l, x in zip(lbds, xs)]
        xs = [x + _dot(x, t) for x, t in zip(xs, ts)]
        s *= 2
    return xs


def _rwkv_kernel(pa_ref, mu_ref, wl_ref, w0_ref, a0_ref, kk_ref, ka_ref, rk_ref, lnw_ref, lnb_ref,
                 o_ref, prev_ref, h_ref, *, tb, c, da):
    ti = pl.program_id(1)
    n_pairs = da // LANES
    n_chunks = tb // c
    width = pa_ref.shape[-1]
    assert 2 * c == LANES

    @pl.when(ti == 0)
    def _():
        prev_ref[...] = jnp.zeros_like(prev_ref)
        h_ref[...] = jnp.zeros_like(h_ref)

    p = pa_ref[0]
    prow = _iota((tb, width), 0)
    prev = jnp.where(prow == 0, prev_ref[...], pltpu.roll(p, 1, 0))
    prev_ref[...] = p[tb - 1:tb, :]
    xs = p + mu_ref[...] * (prev - p)
    r = xs[:, 0:da]
    k = xs[:, da:2 * da]
    v = xs[:, 2 * da:3 * da]
    wa = xs[:, 3 * da:3 * da + LANES]
    lane_t = _iota((tb, LANES), 1)
    wa = jnp.where(lane_t < HEAD_DIM, jnp.tanh(wa), wa)
    lora = _dot(wa, wl_ref[...])
    wpre = w0_ref[...] + lora[:, 0:da]
    w = -(jnp.maximum(-wpre, 0.0) + jnp.log1p(jnp.exp(-jnp.abs(wpre)))) - 0.5
    lw = -jnp.exp(w)
    a = _sigmoid(a0_ref[...] + lora[:, da:2 * da])
    k2 = k * (1.0 + (a - 1.0) * ka_ref[...])
    kk = k * kk_ref[...]

    er = _blk(_iota((LANES, LANES), 0), HEAD_DIM)
    ec = _blk(_iota((LANES, LANES), 1), HEAD_DIM)
    bd_mask = er == ec
    ones_head = bd_mask.astype(BF16)
    diag_eye = (_iota((LANES, LANES), 0) == _iota((LANES, LANES), 1)).astype(F32)

    pairs = [_lanes(i) for i in range(n_pairs)]
    kkn = jnp.concatenate(
        [kk[:, sl] * lax.rsqrt(jnp.maximum(_dot_xr(kk[:, sl] * kk[:, sl], ones_head, 2), 1e-24))
         for sl in pairs], axis=1)
    av = -kkn
    bv = kkn * a

    grp = min(tb, 2 * LANES)
    tr = _iota((grp, grp), 0)
    tc = _iota((grp, grp), 1)
    tri_blk = ((_blk(tr, c) == _blk(tc, c)) & (tc <= tr)).astype(BF16)
    cum = jnp.concatenate([_dot_lx(tri_blk, lw[i:i + grp], 2) for i in range(0, tb, grp)], axis=0)
    eg = jnp.exp(cum)
    rt = r * eg
    at = av * jnp.exp(cum - lw)
    ieg = jnp.exp(-cum)
    kt = k2 * ieg
    bt = bv * ieg

    rr = _iota((2 * c, 2 * c), 0)
    cc = _iota((2 * c, 2 * c), 1)
    same_blk = _blk(rr, c) == _blk(cc, c)
    strict_2c = _rem(cc, c) < _rem(rr, c)
    incl_2c = _rem(cc, c) <= _rem(rr, c)
    m0 = _iota((c, LANES), 1) < HEAD_DIM
    m0w = _rem(_iota((c, 2 * LANES), 1), LANES) < HEAD_DIM
    zeros_c = jnp.zeros((c, LANES), F32)

    items = [(sl, slice(j * c, (j + 1) * c)) for sl in pairs for j in range(n_chunks)]
    lbds, laks, prms, qa2s, vstk, gams, bkbars = [], [], [], [], [], [], []
    for sl, rs in items:
        rc, ac, kc, bc, vc = rt[rs, sl], at[rs, sl], kt[rs, sl], bt[rs, sl], v[rs, sl]
        gam = eg[rs.stop - 1:rs.stop, sl]
        bk = jnp.concatenate([bc, kc], axis=0)
        at0 = jnp.where(m0, ac, 0.0)
        at1 = jnp.where(m0, 0.0, ac)
        qa4 = jnp.concatenate([at0, at1, jnp.where(m0, rc, 0.0), jnp.where(m0, 0.0, rc)], axis=0)
        pm = _dot_nt(qa4, bk)
        xall = jnp.concatenate([pm[0:c], pltpu.roll(pm[c:2 * c], c, 1)], axis=0)
        lbds.append(jnp.where(same_blk & strict_2c, xall, 0.0))
        laks.append(jnp.where((~same_blk) & strict_2c, xall, 0.0))
        prms.append(jnp.where(incl_2c, pm[2 * c:], 0.0))
        qa2s.append(jnp.concatenate([at0, at1], axis=0))
        vstk.append(jnp.concatenate([jnp.where(m0, 0.0, vc), jnp.where(m0, vc, 0.0)], axis=0))
        gams.append(gam)
        bkbars.append(jnp.concatenate([bc * gam, kc * gam], axis=0))

    tinvs = _unit_lower_inverses(lbds, c)
    w1s = [_dot(lak, vs) for lak, vs in zip(laks, vstk)]
    tzs = [_dot(tinv, jnp.concatenate([qa2, w1], axis=1))
           for tinv, qa2, w1 in zip(tinvs, qa2s, w1s)]
    r3s, y2s, phis, gs = [], [], [], []
    for (sl, rs), tz, prm, bkbar, gam in zip(items, tzs, prms, bkbars, gams):
        a2v2 = tz[0:c] + tz[c:2 * c]
        rhs = jnp.concatenate([a2v2, jnp.concatenate([zeros_c, v[rs, sl]], axis=1)], axis=0)
        res = _dot(jnp.concatenate([prm, bkbar.T], axis=0), rhs)
        ry = jnp.where(m0w, res[0:c], res[c:2 * c])
        r3s.append(rt[rs, sl] + ry[:, 0:LANES])
        y2s.append(ry[:, LANES:])
        pg = res[2 * c:]
        phis.append(diag_eye * gam + jnp.where(bd_mask, pg[:, 0:LANES], 0.0))
        gs.append(jnp.where(bd_mask, pg[:, LANES:], 0.0))

    hs = [h_ref[i] for i in range(n_pairs)]
    ys = [[] for _ in range(n_pairs)]
    for j in range(n_chunks):
        for i in range(n_pairs):
            idx = i * n_chunks + j
            ys[i].append(_dot(r3s[idx], hs[i]) + y2s[idx])
        hs = [_dot(phis[i * n_chunks + j], hs[i]) + gs[i * n_chunks + j] for i in range(n_pairs)]
    for i, sl in enumerate(pairs):
        h_ref[i] = hs[i]
        y = jnp.concatenate(ys[i], axis=0)
        mean = _dot_xr(y, ones_head, 2) * (1.0 / HEAD_DIM)
        yc_ = y - mean
        var = _dot_xr(yc_ * yc_, ones_head, 2) * (1.0 / HEAD_DIM)
        yn = yc_ * lax.rsqrt(var + GN_EPS) * lnw_ref[:, sl] + lnb_ref[:, sl]
        bonus = _dot_xr(r[:, sl] * k2[:, sl] * rk_ref[:, sl], ones_head, 2) * v[:, sl]
        o_ref[0, :, sl] = (yn + bonus).astype(o_ref.dtype)


def _rwkv(pa, mu, wl, w0, a0, k_k, k_a, r_k, ln_w, ln_b, tb=512, c=64):
    b, s, width = pa.shape
    da = w0.shape[-1]
    vec = lambda n: pl.BlockSpec((1, n), lambda bi, ti: (0, 0))
    return pl.pallas_call(
        functools.partial(_rwkv_kernel, tb=tb, c=c, da=da),
        grid=(b, s // tb),
        in_specs=[pl.BlockSpec((1, tb, width), lambda bi, ti: (bi, ti, 0)),
                  vec(width),
                  pl.BlockSpec(wl.shape, lambda bi, ti: (0, 0)),
                  vec(da), vec(da), vec(da), vec(da), vec(da), vec(da), vec(da)],
        out_specs=pl.BlockSpec((1, tb, da), lambda bi, ti: (bi, ti, 0)),
        out_shape=jax.ShapeDtypeStruct((b, s, da), BF16),
        scratch_shapes=[pltpu.VMEM((1, width), F32),
                        pltpu.VMEM((da // LANES, LANES, LANES), F32)],
        compiler_params=_cparams(("parallel", "arbitrary")),
        name="rwkv7",
    )(pa, mu, wl, w0, a0, k_k, k_a, r_k, ln_w, ln_b)


def kernel(x, norm_w, w_in, b_f, mu, w0, w_up, a0, a_up, k_k, k_a, r_k, ln_x_w, ln_x_b, w_out,
           final_norm_w):
    b, s, d = x.shape
    depth = w_in.shape[0]
    da = w0.shape[-1]
    lora = w_up.shape[1]
    hc = b_f.shape[-1]
    d_mix = w_out.shape[1]
    db = (d_mix - da) // 2
    n_shift = 3 * da + 2 * lora
    n_qkv = 6 * db
    assert da % LANES == 0 and 2 * lora == LANES and db == hc * HEAD_DIM and hc % 2 == 1
    assert mu.shape[-1] == n_shift and (n_shift + n_qkv) % LANES == 0
    n_out_blocks = (db + HEAD_DIM) // LANES
    dbp = n_out_blocks * LANES
    scale = HEAD_DIM ** -0.5

    col = jnp.arange(n_shift + n_qkv)
    is_q = ((col >= n_shift) & (col < n_shift + db)) | (
        (col >= n_shift + 3 * db) & (col < n_shift + 4 * db))
    col_scale = jnp.where(is_q, scale, 1.0).astype(F32)
    sb_heads = tuple(_Head(q=_slot(HEAD_DIM * g), k=_slot(db + HEAD_DIM * g),
                           v=_slot(2 * db + HEAD_DIM * g), out=_slot(HEAD_DIM * g))
                     for g in range(hc))
    fox_heads = tuple(_Head(q=_slot(3 * db + HEAD_DIM * j), k=_slot(4 * db + HEAD_DIM * j),
                            v=_slot(5 * db + HEAD_DIM * j), out=_slot(HEAD_DIM * (j + 1)), c=j + 1)
                      for j in range(hc))
    sb_width = -(-(3 * db) // (2 * LANES)) * 2 * LANES
    assert all(hd.out[1] == hd.v[1] for hd in sb_heads + fox_heads) and sb_width <= n_qkv

    x2 = x.reshape(b * s, d)
    zpad = lambda rows, n: jnp.zeros((rows, n), F32)
    for l in range(depth):
        wl_in = w_in[l]
        o = n_shift + n_qkv
        wf = wl_in[:, o:o + hc]
        wg = wl_in[:, o + hc:]
        w_all = jnp.concatenate(
            [wl_in[:, :o] * col_scale[None, :],
             zpad(d, 1), wf, zpad(d, LANES - 1 - hc),
             wg[:, :da + db], zpad(d, 2 * HEAD_DIM), wg[:, da + db:]], axis=1).astype(BF16)
        widths = (n_shift, n_qkv, LANES, da + 2 * dbp)
        dtypes = (F32, BF16, F32, BF16)
        pa, qkv, f, g = _inproj(x2, norm_w[l][None, :], w_all, widths, dtypes)

        wl_comb = jnp.concatenate(
            [jnp.concatenate([w_up[l], zpad(lora, da)], axis=1),
             jnp.concatenate([zpad(lora, da), a_up[l]], axis=1)], axis=0).astype(BF16)
        row = lambda t: t.reshape(1, -1)
        ya = _rwkv(pa.reshape(b, s, n_shift), row(mu[l]), wl_comb, row(w0[l]), row(a0[l]),
                   row(k_k[l]), row(k_a[l]), row(r_k[l]), row(ln_x_w[l]), row(ln_x_b[l]))

        qkv3 = qkv.reshape(b, s, n_qkv)
        yb = _sb_attention(qkv3, sb_heads, n_out_blocks, sb_width)
        bf_pad = jnp.concatenate([jnp.zeros((1,), F32), b_f[l], jnp.zeros((LANES - 1 - hc,), F32)])
        cum_f = _fcum(f.reshape(b, s, LANES), bf_pad[None, :])
        yc = _fox_attention(qkv3, cum_f, fox_heads, n_out_blocks)

        wo = w_out[l]
        wo_p = jnp.concatenate([wo[:da + db], zpad(2 * HEAD_DIM, d), wo[da + db:]], axis=0).astype(BF16)
        x2 = _outproj(x2, ya.reshape(b * s, da), yb.reshape(b * s, dbp), yc.reshape(b * s, dbp),
                      g, wo_p, final_norm_w[None, :], final=(l == depth - 1))
    return x2.reshape(b, s, d)
```

```python
import functools
import itertools
from typing import NamedTuple

import jax
import jax.numpy as jnp
from jax import lax
from jax.experimental import pallas as pl
from jax.experimental.pallas import tpu as pltpu

HEAD_DIM = 64
LANES = 128
NORM_EPS = 1e-6
GN_EPS = 64e-5
VMEM_LIMIT = 56 * 1024 * 1024
EXP_ZERO = -104.0
SHIFT_SAFE = 80.0

F32 = jnp.float32
BF16 = jnp.bfloat16


class _Head(NamedTuple):
    q: tuple
    k: tuple
    v: tuple
    out: tuple
    c: int = -1


def _slot(col):
    assert col % HEAD_DIM == 0
    return (col // LANES, (col % LANES) // HEAD_DIM)


def _iota(shape, dim):
    return lax.broadcasted_iota(jnp.int32, shape, dim)


def _blk(idx, size):
    assert size & (size - 1) == 0
    return jnp.right_shift(idx, size.bit_length() - 1)


def _rem(idx, size):
    assert size & (size - 1) == 0
    return jnp.bitwise_and(idx, size - 1)


def _lanes(blk):
    return slice(blk * LANES, (blk + 1) * LANES)


def _dot(a, b):
    return jnp.dot(a.astype(BF16), b.astype(BF16), preferred_element_type=F32)


def _dot_nt(a, b):
    return lax.dot_general(a.astype(BF16), b.astype(BF16), (((1,), (1,)), ((), ())),
                           preferred_element_type=F32)


def _split(x, n):
    parts = []
    rem = x
    for i in range(n):
        p = rem.astype(BF16)
        parts.append(p)
        if i + 1 < n:
            rem = rem - p.astype(F32)
    return parts


def _dot_xr(x, r, n):
    out = None
    for p in _split(x, n):
        t = jnp.dot(p, r, preferred_element_type=F32)
        out = t if out is None else out + t
    return out


def _dot_lx(l, x, n):
    out = None
    for p in _split(x, n):
        t = jnp.dot(l, p, preferred_element_type=F32)
        out = t if out is None else out + t
    return out


def _log_sigmoid(z):
    return jnp.minimum(z, 0.0) - jnp.log1p(jnp.exp(-jnp.abs(z)))


def _sigmoid(z):
    return 1.0 / (1.0 + jnp.exp(-z))


def _head_lanes(lane, h):
    return (lane >= h * HEAD_DIM) & (lane < (h + 1) * HEAD_DIM)


def _query_in_key_half(q2, lane, hd):
    own = jnp.where(_head_lanes(lane, hd.q[1]), q2, jnp.zeros_like(q2))
    if hd.q[1] != hd.k[1]:
        own = pltpu.roll(own.astype(F32), HEAD_DIM, 1).astype(BF16)
    return own


def _write_heads(o_ref, rows, lane, heads, outs):
    t = lane.shape[0]
    for blk in sorted({hd.out[0] for hd in heads}):
        halves = {}
        for hd, o in zip(heads, outs):
            assert hd.out[1] == hd.v[1]
            if hd.out[0] == blk:
                halves[hd.out[1]] = o
        zero = jnp.zeros((t, LANES), F32)
        o_ref[0, rows, _lanes(blk)] = jnp.where(
            lane < HEAD_DIM, halves.get(0, zero), halves.get(1, zero)).astype(o_ref.dtype)


def _cparams(sem):
    return pltpu.CompilerParams(dimension_semantics=sem, vmem_limit_bytes=VMEM_LIMIT)


def _wprep_in_kernel(w_ref, o_ref, tail_ref, *, n_main, q_ranges, hc, g_split, g_gap):
    n_in = w_ref.shape[-1]
    n_full = (n_in - n_main) // LANES * LANES
    n_tail = n_in - n_main - n_full
    n_src = n_full + LANES
    n_dst = o_ref.shape[-1] - n_main
    col = _iota((1, n_main), 1)
    is_q = None
    for a, b in q_ranges:
        m = (col >= a) & (col < b)
        is_q = m if is_q is None else (is_q | m)
    scale = jnp.where(is_q, HEAD_DIM ** -0.5, 1.0).astype(F32)
    o_ref[0, :, 0:n_main] = (w_ref[0, :, 0:n_main] * scale).astype(BF16)
    tail_ref[...] = jnp.zeros_like(tail_ref)
    tail_ref[:, 0:n_tail] = w_ref[0, :, n_main + n_full:n_in]
    x = jnp.concatenate([w_ref[0, :, n_main:n_main + n_full], tail_ref[...]], axis=1).astype(BF16)
    src = _iota((n_src, n_dst), 0)
    dst = _iota((n_src, n_dst), 1)
    g = src - hc
    dmap = jnp.where(src < hc, src + 1,
                     jnp.where(g < g_split, g + LANES, g + LANES + g_gap))
    move = ((dst == dmap) & (src < n_in - n_main)).astype(BF16)
    o_ref[0, :, n_main:] = jnp.dot(x, move, preferred_element_type=F32).astype(BF16)


def _wprep_in(w_in, n_main, q_ranges, hc, g_split, g_gap, tr=256):
    depth, d, n_in = w_in.shape
    n_out = n_main + LANES + (n_in - n_main - hc) + g_gap
    assert n_main % LANES == 0 and n_out % LANES == 0 and d % tr == 0
    return pl.pallas_call(
        functools.partial(_wprep_in_kernel, n_main=n_main, q_ranges=q_ranges, hc=hc,
                          g_split=g_split, g_gap=g_gap),
        grid=(depth, d // tr),
        in_specs=[pl.BlockSpec((1, tr, n_in), lambda l, i: (l, i, 0))],
        out_specs=pl.BlockSpec((1, tr, n_out), lambda l, i: (l, i, 0)),
        out_shape=jax.ShapeDtypeStruct((depth, d, n_out), BF16),
        scratch_shapes=[pltpu.VMEM((tr, LANES), F32)],
        compiler_params=_cparams(("parallel", "parallel")),
        name="wprep_in",
    )(w_in)


def _wprep_out_kernel(w_ref, o_ref, *, split, gap):
    o_ref[0, 0:split, :] = w_ref[0, 0:split, :].astype(BF16)
    o_ref[0, split:split + gap, :] = jnp.zeros((gap, o_ref.shape[-1]), BF16)
    o_ref[0, split + gap:, :] = w_ref[0, split:, :].astype(BF16)


def _wprep_out(w_out, split, gap):
    depth, k, d = w_out.shape
    return pl.pallas_call(
        functools.partial(_wprep_out_kernel, split=split, gap=gap),
        grid=(depth,),
        in_specs=[pl.BlockSpec((1, k, d), lambda l: (l, 0, 0))],
        out_specs=pl.BlockSpec((1, k + gap, d), lambda l: (l, 0, 0)),
        out_shape=jax.ShapeDtypeStruct((depth, k + gap, d), BF16),
        compiler_params=_cparams(("parallel",)),
        name="wprep_out",
    )(w_out)


def _inproj_kernel(x_ref, nw_ref, w_ref, *out_refs, n_chunk):
    x = x_ref[...]
    xn = x * lax.rsqrt(jnp.mean(x * x, axis=-1, keepdims=True) + NORM_EPS)
    h = (xn * nw_ref[...]).astype(BF16)
    outs, off = [], 0
    for ref in out_refs:
        outs.append((ref, off, off + ref.shape[-1]))
        off += ref.shape[-1]
    for c0 in range(0, off, n_chunk):
        c1 = min(c0 + n_chunk, off)
        res = jnp.dot(h, w_ref[:, c0:c1], preferred_element_type=F32)
        for ref, o0, o1 in outs:
            a, b = max(c0, o0), min(c1, o1)
            if a < b:
                ref[:, a - o0:b - o0] = res[:, a - c0:b - c0].astype(ref.dtype)


def _inproj(x2, nw, w, layer, widths, dtypes, tm=512, n_chunk=512):
    m, d = x2.shape
    n = w.shape[2]
    assert sum(widths) == n and m % tm == 0
    out_shape = [jax.ShapeDtypeStruct((m, wd), dt) for wd, dt in zip(widths, dtypes)]
    out_specs = [pl.BlockSpec((tm, wd), lambda i: (i, 0)) for wd in widths]
    return pl.pallas_call(
        functools.partial(_inproj_kernel, n_chunk=n_chunk),
        grid=(m // tm,),
        in_specs=[pl.BlockSpec((tm, d), lambda i: (i, 0)),
                  pl.BlockSpec((1, d), lambda i: (0, 0)),
                  pl.BlockSpec((None, d, n), lambda i: (layer, 0, 0),
                               pipeline_mode=pl.Buffered(1))],
        out_specs=out_specs,
        out_shape=out_shape,
        compiler_params=_cparams(("parallel",)),
        name="inproj",
    )(x2, nw, w)


def _outproj_kernel(x_ref, ya_ref, yb_ref, yc_ref, g_ref, w_ref, fw_ref, o_ref, *, final):
    g = g_ref[...].astype(F32)
    y = jnp.concatenate([ya_ref[...], yb_ref[...], yc_ref[...]], axis=1).astype(F32)
    yg = (y * (g * _sigmoid(g))).astype(BF16)
    acc = x_ref[...] + jnp.dot(yg, w_ref[...], preferred_element_type=F32)
    if final:
        acc = acc * lax.rsqrt(jnp.mean(acc * acc, axis=-1, keepdims=True) + NORM_EPS) * fw_ref[...]
    o_ref[...] = acc


def _outproj(x2, ya, yb, yc, g, w, layer, fw, final, tm=1024):
    m, d = x2.shape
    row = lambda wd: pl.BlockSpec((tm, wd), lambda i: (i, 0))
    return pl.pallas_call(
        functools.partial(_outproj_kernel, final=final),
        grid=(m // tm,),
        in_specs=[row(d), row(ya.shape[1]), row(yb.shape[1]), row(yc.shape[1]), row(g.shape[1]),
                  pl.BlockSpec((None,) + w.shape[1:], lambda i: (layer, 0, 0)),
                  pl.BlockSpec((1, d), lambda i: (0, 0))],
        out_specs=row(d),
        out_shape=jax.ShapeDtypeStruct((m, d), F32),
        compiler_params=_cparams(("parallel",)),
        name="outproj",
    )(x2, ya, yb, yc, g, w, fw)


def _fcum_kernel(f_ref, bf_ref, c_ref, *, blk):
    s = f_ref.shape[1]
    row = _iota((blk, blk), 0)
    col = _iota((blk, blk), 1)
    tri = (col <= row).astype(BF16)
    carry = jnp.zeros((1, LANES), F32)
    for i in range(s // blk):
        lf = _log_sigmoid(f_ref[0, i * blk:(i + 1) * blk, :] + bf_ref[...])
        c = _dot_lx(tri, lf, 3) + carry
        c_ref[0, i * blk:(i + 1) * blk, :] = c
        carry = c[blk - 1:blk, :]


def _fcum(f, bf, blk=256):
    b, s, _ = f.shape
    return pl.pallas_call(
        functools.partial(_fcum_kernel, blk=blk),
        grid=(b,),
        in_specs=[pl.BlockSpec((1, s, LANES), lambda i: (i, 0, 0)),
                  pl.BlockSpec((1, LANES), lambda i: (0, 0))],
        out_specs=pl.BlockSpec((1, s, LANES), lambda i: (i, 0, 0)),
        out_shape=jax.ShapeDtypeStruct((b, s, LANES), F32),
        compiler_params=_cparams(("parallel",)),
        name="fcum",
    )(f, bf)


class _SbBlock:
    def __init__(self, q_ref, kv_ref, o_ref, qi, rows, heads):
        t = rows.stop - rows.start
        self.q_ref, self.kv_ref, self.o_ref = q_ref, kv_ref, o_ref
        self.qi, self.rows, self.heads, self.t = qi, rows, heads, t
        self.lane = _iota((t, LANES), 1)
        row = _iota((t, t), 0)
        col = _iota((t, t), 1)
        self.strict = col < row
        self.tri = (row > col).astype(BF16)

    def kv(self, k0, hd):
        rows = pl.ds(k0, self.t)
        return self.kv_ref[0, rows, _lanes(hd.k[0])], self.kv_ref[0, rows, _lanes(hd.v[0])]

    def logits(self, qh, k2, diag):
        z = _dot_nt(qh, k2)
        ls = jnp.minimum(z, 0.0) - jnp.log(1.0 + jnp.exp(-jnp.abs(z)))
        log1m = ls - z
        if diag:
            log1m = jnp.where(self.strict, log1m, 0.0)
        return ls, _dot_xr(log1m, self.tri, 1), jnp.sum(log1m, axis=-1, keepdims=True)

    def front(self):
        t, qi = self.t, self.qi
        q0 = pl.multiple_of(qi * t, t)
        p0 = pl.multiple_of(jnp.maximum(qi - 1, 0) * t, t)
        no_prev = jnp.where(qi > 0, 0.0, -1e30).astype(F32)
        self.qhs, self.accs, self.carries = [], [], []
        for hd in self.heads:
            qh = _query_in_key_half(self.q_ref[0, self.rows, _lanes(hd.q[0])], self.lane, hd)
            kd, vd = self.kv(q0, hd)
            kp, vp = self.kv(p0, hd)
            ls_d, after_d, mass_d = self.logits(qh, kd, True)
            ls_p, after_p, mass_p = self.logits(qh, kp, False)
            attn_d = jnp.where(self.strict, jnp.exp(ls_d + after_d), 0.0)
            attn_p = jnp.exp(ls_p + after_p + (mass_d + no_prev))
            self.qhs.append(qh)
            self.accs.append(_dot(attn_d, vd) + _dot(attn_p, vp))
            self.carries.append(mass_d + mass_p)
            yield

    def finish(self):
        t = self.t

        def any_live(carries):
            m = carries[0]
            for c in carries[1:]:
                m = jnp.maximum(m, c)
            return (jnp.max(m) > EXP_ZERO).astype(jnp.int32)

        def cond(st):
            return (st[0] >= 0) & (st[1] > 0)

        def body(st):
            kb, _, accs, carries = st
            k0 = pl.multiple_of(kb * t, t)
            new_accs, new_carries = [], []
            for hd, qh, acc, carry in zip(self.heads, self.qhs, accs, carries):
                k2, v2 = self.kv(k0, hd)
                ls, after, mass = self.logits(qh, k2, False)
                new_accs.append(acc + _dot(jnp.exp(ls + after + carry), v2))
                new_carries.append(carry + mass)
            return kb - 1, any_live(new_carries), tuple(new_accs), tuple(new_carries)

        st = lax.while_loop(cond, body, (self.qi - 2, any_live(self.carries), tuple(self.accs),
                                         tuple(self.carries)))
        _write_heads(self.o_ref, self.rows, self.lane, self.heads, st[2])


def _bias_lanes(lane, base, own, vals, ones_first):
    slots = ([None] * 3 + list(vals)) if ones_first else (list(vals) + [None] * 3)
    out = own
    for i, val in enumerate(slots):
        fill = jnp.ones((), BF16) if val is None else val
        out = jnp.where(lane == base + i, fill, out)
    return out


def _fox_kernel(q_ref, kv_ref, cq_ref, ck_ref, o_ref, kaug_ref, vaug_ref, acc_ref, kmax_ref,
                *, t, heads):
    qi = pl.program_id(1)
    s_len = kv_ref.shape[1]
    lane = _iota((t, LANES), 1)
    causal = _iota((t, t), 1) <= _iota((t, t), 0)
    q0 = pl.multiple_of(qi * t, t)
    bias_base = [HEAD_DIM * (1 - hd.k[1]) for hd in heads]
    ones_lane = [HEAD_DIM * (1 - hd.v[1]) for hd in heads]

    @pl.when(qi == 0)
    def _():
        lane_s = _iota((s_len, LANES), 1)
        call = ck_ref[0]
        for i, hd in enumerate(heads):
            k2 = kv_ref[0, :, _lanes(hd.k[0])]
            kf = jnp.where(_head_lanes(lane_s, hd.k[1]), k2, jnp.zeros_like(k2)).astype(F32)
            kmax_ref[i] = jnp.max(jnp.sqrt(jnp.sum(kf * kf, axis=-1, keepdims=True)))
            ck = jnp.sum(jnp.where(lane_s == hd.c, call, 0.0), axis=-1, keepdims=True)
            kaug_ref[i] = _bias_lanes(lane_s, bias_base[i], k2, [-p for p in _split(ck, 3)],
                                      ones_first=True)
            vaug_ref[i] = jnp.where(lane_s == ones_lane[i], jnp.ones((), BF16),
                                    kv_ref[0, :, _lanes(hd.v[0])])

    qas, ubound = [], None
    for i, hd in enumerate(heads):
        own = _query_in_key_half(q_ref[0, :, _lanes(hd.q[0])], lane, hd)
        qf = own.astype(F32)
        u = jnp.sqrt(jnp.sum(qf * qf, axis=-1, keepdims=True)) * kmax_ref[i]
        ubound = u if ubound is None else jnp.maximum(ubound, u)
        cq = jnp.sum(jnp.where(lane == hd.c, cq_ref[0], 0.0), axis=-1, keepdims=True)
        qas.append(_bias_lanes(lane, bias_base[i], own, _split(cq - u, 3), ones_first=False))
    safe = 2.0 * jnp.max(ubound) < SHIFT_SAFE

    def logits(i, k0, diag):
        s_blk = _dot_nt(qas[i], kaug_ref[i, pl.ds(k0, t), :])
        return jnp.where(causal, s_blk, -1e30) if diag else s_blk

    @pl.when(safe)
    def _():
        def sweep(k0, diag):
            for i in range(len(heads)):
                p = jnp.exp(logits(i, k0, diag))
                acc_ref[i] += _dot(p, vaug_ref[i, pl.ds(k0, t), :])

        acc_ref[...] = jnp.zeros_like(acc_ref)

        @pl.loop(0, qi)
        def _(j):
            sweep(pl.multiple_of(j * t, t), False)

        sweep(q0, True)
        outs = []
        for i in range(len(heads)):
            acc = acc_ref[i]
            l = jnp.sum(jnp.where(lane == ones_lane[i], acc, 0.0), axis=-1, keepdims=True)
            outs.append(acc * (1.0 / l))
        _write_heads(o_ref, slice(None), lane, heads, outs)

    @pl.when(jnp.logical_not(safe))
    def _():
        def update(st, s_blk, v2):
            m, l, acc = st
            m_new = jnp.maximum(m, jnp.max(s_blk, axis=-1, keepdims=True))
            alpha = jnp.exp(m - m_new)
            p = jnp.exp(s_blk - m_new)
            l = alpha * l + jnp.sum(p, axis=-1, keepdims=True)
            acc = alpha * acc + _dot(p, v2)
            return m_new, l, acc

        def sweep(k0, sts, diag):
            return tuple(update(st, logits(i, k0, diag), vaug_ref[i, pl.ds(k0, t), :])
                         for i, st in enumerate(sts))

        init = tuple((jnp.full((t, 1), -1e30, F32), jnp.zeros((t, 1), F32),
                      jnp.zeros((t, LANES), F32)) for _ in heads)
        sts = lax.fori_loop(0, qi, lambda j, sts: sweep(pl.multiple_of(j * t, t), sts, False),
                            init)
        sts = sweep(q0, sts, True)
        _write_heads(o_ref, slice(None), lane, heads, [acc * (1.0 / l) for _, l, acc in sts])


def _fox_attention(qkv, c, heads, n_out_blocks, t=512):
    b, s, w = qkv.shape
    return pl.pallas_call(
        functools.partial(_fox_kernel, t=t, heads=heads),
        grid=(b, s // t),
        in_specs=[pl.BlockSpec((1, t, w), lambda bi, qi: (bi, qi, 0)),
                  pl.BlockSpec((1, s, w), lambda bi, qi: (bi, 0, 0)),
                  pl.BlockSpec((1, t, LANES), lambda bi, qi: (bi, qi, 0)),
                  pl.BlockSpec((1, s, LANES), lambda bi, qi: (bi, 0, 0))],
        out_specs=pl.BlockSpec((1, t, n_out_blocks * LANES), lambda bi, qi: (bi, qi, 0)),
        out_shape=jax.ShapeDtypeStruct((b, s, n_out_blocks * LANES), BF16),
        scratch_shapes=[pltpu.VMEM((len(heads), s, LANES), BF16),
                        pltpu.VMEM((len(heads), s, LANES), BF16),
                        pltpu.VMEM((len(heads), t, LANES), F32),
                        pltpu.SMEM((len(heads),), F32)],
        compiler_params=_cparams(("parallel", "arbitrary")),
        name="fox",
    )(qkv, qkv, c, c)


def _unit_lower_inverses(lbds, c):
    n = lbds[0].shape[0]
    row = _iota((n, n), 0)
    col = _iota((n, n), 1)
    eye = (row == col).astype(F32)
    same2 = _blk(row, 2) == _blk(col, 2)
    xs = [eye + jnp.where(same2, l, 0.0) for l in lbds]
    s = 2
    while s < c:
        join = (_blk(row, 2 * s) == _blk(col, 2 * s)) & (_blk(row, s) != _blk(col, s))
        ts = [_dot(jnp.where(join, l, 0.0), x) for l, x in zip(lbds, xs)]
        yield True
        xs = [x + _dot(x, t) for x, t in zip(xs, ts)]
        yield True
        s *= 2
    return xs


def _rwkv_body(pa_ref, mu_ref, wl_ref, w0_ref, a0_ref, kk_ref, ka_ref, rk_ref, lnw_ref, lnb_ref,
               o_ref, prev_ref, h_ref, *, tb, c, da):
    n_pairs = da // LANES
    n_chunks = tb // c
    width = pa_ref.shape[-1]
    assert 2 * c == LANES

    p = pa_ref[0]
    prow = _iota((tb, width), 0)
    prev = jnp.where(prow == 0, prev_ref[...], pltpu.roll(p, 1, 0))
    prev_ref[...] = p[tb - 1:tb, :]
    xs = p + mu_ref[...] * (prev - p)
    r = xs[:, 0:da]
    k = xs[:, da:2 * da]
    v = xs[:, 2 * da:3 * da]
    wa = xs[:, 3 * da:3 * da + LANES]
    lane_t = _iota((tb, LANES), 1)
    wa = jnp.where(lane_t < HEAD_DIM, jnp.tanh(wa), wa)
    lora = _dot(wa, wl_ref[...])
    wpre = w0_ref[...] + lora[:, 0:da]
    w = -(jnp.maximum(-wpre, 0.0) + jnp.log1p(jnp.exp(-jnp.abs(wpre)))) - 0.5
    lw = -jnp.exp(w)
    a = _sigmoid(a0_ref[...] + lora[:, da:2 * da])
    k2 = k * (1.0 + (a - 1.0) * ka_ref[...])
    kk = k * kk_ref[...]

    er = _blk(_iota((LANES, LANES), 0), HEAD_DIM)
    ec = _blk(_iota((LANES, LANES), 1), HEAD_DIM)
    bd_mask = er == ec
    ones_head = bd_mask.astype(BF16)
    diag_eye = (_iota((LANES, LANES), 0) == _iota((LANES, LANES), 1)).astype(F32)

    pairs = [_lanes(i) for i in range(n_pairs)]
    kkn = jnp.concatenate(
        [kk[:, sl] * lax.rsqrt(jnp.maximum(_dot_xr(kk[:, sl] * kk[:, sl], ones_head, 2), 1e-24))
         for sl in pairs], axis=1)
    av = -kkn
    bv = kkn * a

    grp = min(tb, 2 * LANES)
    tr = _iota((grp, grp), 0)
    tc = _iota((grp, grp), 1)
    tri_blk = ((_blk(tr, c) == _blk(tc, c)) & (tc <= tr)).astype(BF16)
    cum = jnp.concatenate([_dot_lx(tri_blk, lw[i:i + grp], 2) for i in range(0, tb, grp)], axis=0)
    eg = jnp.exp(cum)
    rt = r * eg
    at = av * jnp.exp(cum - lw)
    ieg = jnp.exp(-cum)
    kt = k2 * ieg
    bt = bv * ieg

    rr = _iota((2 * c, 2 * c), 0)
    cc = _iota((2 * c, 2 * c), 1)
    same_blk = _blk(rr, c) == _blk(cc, c)
    strict_2c = _rem(cc, c) < _rem(rr, c)
    incl_2c = _rem(cc, c) <= _rem(rr, c)
    m0 = _iota((c, LANES), 1) < HEAD_DIM
    m0w = _rem(_iota((c, 2 * LANES), 1), LANES) < HEAD_DIM
    zeros_c = jnp.zeros((c, LANES), F32)

    items = [(sl, slice(j * c, (j + 1) * c)) for sl in pairs for j in range(n_chunks)]
    lbds, laks, prms, qa2s, vstk, gams, bkbars = [], [], [], [], [], [], []
    for sl, rs in items:
        rc, ac, kc, bc, vc = rt[rs, sl], at[rs, sl], kt[rs, sl], bt[rs, sl], v[rs, sl]
        gam = eg[rs.stop - 1:rs.stop, sl]
        bk = jnp.concatenate([bc, kc], axis=0)
        at0 = jnp.where(m0, ac, 0.0)
        at1 = jnp.where(m0, 0.0, ac)
        qa4 = jnp.concatenate([at0, at1, jnp.where(m0, rc, 0.0), jnp.where(m0, 0.0, rc)], axis=0)
        pm = _dot_nt(qa4, bk)
        xall = jnp.concatenate([pm[0:c], pltpu.roll(pm[c:2 * c], c, 1)], axis=0)
        lbds.append(jnp.where(same_blk & strict_2c, xall, 0.0))
        laks.append(jnp.where((~same_blk) & strict_2c, xall, 0.0))
        prms.append(jnp.where(incl_2c, pm[2 * c:], 0.0))
        qa2s.append(jnp.concatenate([at0, at1], axis=0))
        vstk.append(jnp.concatenate([jnp.where(m0, 0.0, vc), jnp.where(m0, vc, 0.0)], axis=0))
        gams.append(gam)
        bkbars.append(jnp.concatenate([bc * gam, kc * gam], axis=0))

    yield True
    tinvs = yield from _unit_lower_inverses(lbds, c)
    w1s = [_dot(lak, vs) for lak, vs in zip(laks, vstk)]
    tzs = [_dot(tinv, jnp.concatenate([qa2, w1], axis=1))
           for tinv, qa2, w1 in zip(tinvs, qa2s, w1s)]
    r3s, y2s, phis, gs = [], [], [], []
    for (sl, rs), tz, prm, bkbar, gam in zip(items, tzs, prms, bkbars, gams):
        a2v2 = tz[0:c] + tz[c:2 * c]
        rhs = jnp.concatenate([a2v2, jnp.concatenate([zeros_c, v[rs, sl]], axis=1)], axis=0)
        res = _dot(jnp.concatenate([prm, bkbar.T], axis=0), rhs)
        ry = jnp.where(m0w, res[0:c], res[c:2 * c])
        r3s.append(rt[rs, sl] + ry[:, 0:LANES])
        y2s.append(ry[:, LANES:])
        pg = res[2 * c:]
        phis.append(diag_eye * gam + jnp.where(bd_mask, pg[:, 0:LANES], 0.0))
        gs.append(jnp.where(bd_mask, pg[:, LANES:], 0.0))

    hs = [h_ref[i] for i in range(n_pairs)]
    ys = [[] for _ in range(n_pairs)]
    for j in range(n_chunks):
        for i in range(n_pairs):
            idx = i * n_chunks + j
            ys[i].append(_dot(r3s[idx], hs[i]) + y2s[idx])
        hs = [_dot(phis[i * n_chunks + j], hs[i]) + gs[i * n_chunks + j] for i in range(n_pairs)]
    for i, sl in enumerate(pairs):
        h_ref[i] = hs[i]
        y = jnp.concatenate(ys[i], axis=0)
        mean = _dot_xr(y, ones_head, 2) * (1.0 / HEAD_DIM)
        yc_ = y - mean
        var = _dot_xr(yc_ * yc_, ones_head, 2) * (1.0 / HEAD_DIM)
        yn = yc_ * lax.rsqrt(var + GN_EPS) * lnw_ref[:, sl] + lnb_ref[:, sl]
        bonus = _dot_xr(r[:, sl] * k2[:, sl] * rk_ref[:, sl], ones_head, 2) * v[:, sl]
        o_ref[0, :, sl] = (yn + bonus).astype(o_ref.dtype)


def _rwkv_sb_kernel(*refs, tb, c, da, t, heads):
    rwkv_in, (q_ref, kv_ref, ya_ref, yb_ref, prev_ref, h_ref) = refs[:10], refs[10:]
    ti = pl.program_id(1)

    @pl.when(ti == 0)
    def _():
        prev_ref[...] = jnp.zeros_like(prev_ref)
        h_ref[...] = jnp.zeros_like(h_ref)

    blocks = [_SbBlock(q_ref, kv_ref, yb_ref, ti * (tb // t) + j, slice(j * t, (j + 1) * t), heads)
              for j in range(tb // t)]
    sb_heads = itertools.chain(*[blk.front() for blk in blocks])
    for mxu_run_next in _rwkv_body(*rwkv_in, ya_ref, prev_ref, h_ref, tb=tb, c=c, da=da):
        if mxu_run_next:
            next(sb_heads, None)
    for _ in sb_heads:
        pass
    for blk in blocks:
        blk.finish()


def _rwkv_sb(pa, mu, wl, w0, a0, k_k, k_a, r_k, ln_w, ln_b, qkv, heads, n_out_blocks, sb_width,
             tb=512, c=64, t=256):
    b, s, width = pa.shape
    da = w0.shape[-1]
    vec = lambda n: pl.BlockSpec((1, n), lambda bi, ti: (0, 0))
    return pl.pallas_call(
        functools.partial(_rwkv_sb_kernel, tb=tb, c=c, da=da, t=t, heads=heads),
        grid=(b, s // tb),
        in_specs=[pl.BlockSpec((1, tb, width), lambda bi, ti: (bi, ti, 0)),
                  vec(width),
                  pl.BlockSpec(wl.shape, lambda bi, ti: (0, 0)),
                  vec(da), vec(da), vec(da), vec(da), vec(da), vec(da), vec(da),
                  pl.BlockSpec((1, tb, sb_width), lambda bi, ti: (bi, ti, 0)),
                  pl.BlockSpec((1, s, sb_width), lambda bi, ti: (bi, 0, 0))],
        out_specs=[pl.BlockSpec((1, tb, da), lambda bi, ti: (bi, ti, 0)),
                   pl.BlockSpec((1, tb, n_out_blocks * LANES), lambda bi, ti: (bi, ti, 0))],
        out_shape=[jax.ShapeDtypeStruct((b, s, da), BF16),
                   jax.ShapeDtypeStruct((b, s, n_out_blocks * LANES), BF16)],
        scratch_shapes=[pltpu.VMEM((1, width), F32),
                        pltpu.VMEM((da // LANES, LANES, LANES), F32)],
        compiler_params=_cparams(("parallel", "arbitrary")),
        name="rwkv7_stickbreak",
    )(pa, mu, wl, w0, a0, k_k, k_a, r_k, ln_w, ln_b, qkv, qkv)


def kernel(x, norm_w, w_in, b_f, mu, w0, w_up, a0, a_up, k_k, k_a, r_k, ln_x_w, ln_x_b, w_out,
           final_norm_w):
    b, s, d = x.shape
    depth = w_in.shape[0]
    da = w0.shape[-1]
    lora = w_up.shape[1]
    hc = b_f.shape[-1]
    d_mix = w_out.shape[1]
    db = (d_mix - da) // 2
    n_shift = 3 * da + 2 * lora
    n_qkv = 6 * db
    assert da % LANES == 0 and 2 * lora == LANES and db == hc * HEAD_DIM and hc % 2 == 1
    assert mu.shape[-1] == n_shift and (n_shift + n_qkv) % LANES == 0
    n_out_blocks = (db + HEAD_DIM) // LANES
    dbp = n_out_blocks * LANES

    n_main = n_shift + n_qkv
    q_ranges = ((n_shift, n_shift + db), (n_shift + 3 * db, n_shift + 4 * db))
    w_all = _wprep_in(w_in, n_main, q_ranges, hc, g_split=da + db, g_gap=2 * HEAD_DIM)
    wo_all = _wprep_out(w_out, split=da + db, gap=2 * HEAD_DIM)
    sb_heads = tuple(_Head(q=_slot(HEAD_DIM * g), k=_slot(db + HEAD_DIM * g),
                           v=_slot(2 * db + HEAD_DIM * g), out=_slot(HEAD_DIM * g))
                     for g in range(hc))
    fox_heads = tuple(_Head(q=_slot(3 * db + HEAD_DIM * j), k=_slot(4 * db + HEAD_DIM * j),
                            v=_slot(5 * db + HEAD_DIM * j), out=_slot(HEAD_DIM * (j + 1)), c=j + 1)
                      for j in range(hc))
    sb_width = -(-(3 * db) // (2 * LANES)) * 2 * LANES
    assert all(hd.out[1] == hd.v[1] for hd in sb_heads + fox_heads) and sb_width <= n_qkv

    x2 = x.reshape(b * s, d)
    zpad = lambda rows, n: jnp.zeros((rows, n), F32)
    for l in range(depth):
        widths = (n_shift, n_qkv, LANES, da + 2 * dbp)
        dtypes = (F32, BF16, F32, BF16)
        pa, qkv, f, g = _inproj(x2, norm_w[l][None, :], w_all, l, widths, dtypes)

        wl_comb = jnp.concatenate(
            [jnp.concatenate([w_up[l], zpad(lora, da)], axis=1),
             jnp.concatenate([zpad(lora, da), a_up[l]], axis=1)], axis=0).astype(BF16)
        row = lambda t: t.reshape(1, -1)
        qkv3 = qkv.reshape(b, s, n_qkv)
        ya, yb = _rwkv_sb(pa.reshape(b, s, n_shift), row(mu[l]), wl_comb, row(w0[l]), row(a0[l]),
                          row(k_k[l]), row(k_a[l]), row(r_k[l]), row(ln_x_w[l]), row(ln_x_b[l]),
                          qkv3, sb_heads, n_out_blocks, sb_width)
        bf_pad = jnp.concatenate([jnp.zeros((1,), F32), b_f[l], jnp.zeros((LANES - 1 - hc,), F32)])
        cum_f = _fcum(f.reshape(b, s, LANES), bf_pad[None, :])
        yc = _fox_attention(qkv3, cum_f, fox_heads, n_out_blocks)

        x2 = _outproj(x2, ya.reshape(b * s, da), yb.reshape(b * s, dbp), yc.reshape(b * s, dbp),
                      g, wo_all, l, final_norm_w[None, :], final=(l == depth - 1))
    return x2.reshape(b, s, d)
```

```python
import functools
import itertools
from typing import NamedTuple

import jax
import jax.numpy as jnp
from jax import lax
from jax.experimental import pallas as pl
from jax.experimental.pallas import tpu as pltpu

HEAD_DIM = 64
LANES = 128
NORM_EPS = 1e-6
GN_EPS = 64e-5
VMEM_LIMIT = 56 * 1024 * 1024
EXP_ZERO = -104.0
SHIFT_SAFE = 80.0

F32 = jnp.float32
BF16 = jnp.bfloat16


class _Head(NamedTuple):
    q: tuple
    k: tuple
    v: tuple
    out: tuple
    c: int = -1


def _slot(col):
    assert col % HEAD_DIM == 0
    return (col // LANES, (col % LANES) // HEAD_DIM)


def _iota(shape, dim):
    return lax.broadcasted_iota(jnp.int32, shape, dim)


def _blk(idx, size):
    assert size & (size - 1) == 0
    return jnp.right_shift(idx, size.bit_length() - 1)


def _rem(idx, size):
    assert size & (size - 1) == 0
    return jnp.bitwise_and(idx, size - 1)


def _lanes(blk):
    return slice(blk * LANES, (blk + 1) * LANES)


def _dot(a, b):
    return jnp.dot(a.astype(BF16), b.astype(BF16), preferred_element_type=F32)


def _dot_nt(a, b):
    return lax.dot_general(a.astype(BF16), b.astype(BF16), (((1,), (1,)), ((), ())),
                           preferred_element_type=F32)


def _split(x, n):
    parts = []
    rem = x
    for i in range(n):
        p = rem.astype(BF16)
        parts.append(p)
        if i + 1 < n:
            rem = rem - p.astype(F32)
    return parts


def _dot_xr(x, r, n):
    out = None
    for p in _split(x, n):
        t = jnp.dot(p, r, preferred_element_type=F32)
        out = t if out is None else out + t
    return out


def _dot_lx(l, x, n):
    out = None
    for p in _split(x, n):
        t = jnp.dot(l, p, preferred_element_type=F32)
        out = t if out is None else out + t
    return out


def _log_sigmoid(z):
    return jnp.minimum(z, 0.0) - jnp.log1p(jnp.exp(-jnp.abs(z)))


def _sigmoid(z):
    return 1.0 / (1.0 + jnp.exp(-z))


def _head_lanes(lane, h):
    return (lane >= h * HEAD_DIM) & (lane < (h + 1) * HEAD_DIM)


def _query_in_key_half(q2, lane, hd):
    own = jnp.where(_head_lanes(lane, hd.q[1]), q2, jnp.zeros_like(q2))
    if hd.q[1] != hd.k[1]:
        own = pltpu.roll(own.astype(F32), HEAD_DIM, 1).astype(BF16)
    return own


def _write_heads(o_ref, rows, lane, heads, outs):
    t = lane.shape[0]
    for blk in sorted({hd.out[0] for hd in heads}):
        halves = {}
        for hd, o in zip(heads, outs):
            assert hd.out[1] == hd.v[1]
            if hd.out[0] == blk:
                halves[hd.out[1]] = o
        zero = jnp.zeros((t, LANES), F32)
        o_ref[0, rows, _lanes(blk)] = jnp.where(
            lane < HEAD_DIM, halves.get(0, zero), halves.get(1, zero)).astype(o_ref.dtype)


def _cparams(sem):
    return pltpu.CompilerParams(dimension_semantics=sem, vmem_limit_bytes=VMEM_LIMIT)


def _wprep_in_kernel(w_ref, o_ref, tail_ref, *, n_main, q_ranges, f_bases):
    hc = len(f_bases)
    n_in = w_ref.shape[-1]
    n_full = (n_in - n_main) // LANES * LANES
    n_tail = n_in - n_main - n_full
    n_src = n_full + LANES
    n_dst = o_ref.shape[-1] - n_main
    col = _iota((1, n_main), 1)
    is_q = None
    for a, b in q_ranges:
        m = (col >= a) & (col < b)
        is_q = m if is_q is None else (is_q | m)
    scale = jnp.where(is_q, HEAD_DIM ** -0.5, 1.0).astype(F32)
    o_ref[0, :, 0:n_main] = (w_ref[0, :, 0:n_main] * scale).astype(BF16)
    tail_ref[...] = jnp.zeros_like(tail_ref)
    tail_ref[:, 0:n_tail] = w_ref[0, :, n_main + n_full:n_in]
    x = jnp.concatenate([w_ref[0, :, n_main:n_main + n_full], tail_ref[...]], axis=1).astype(BF16)
    src = _iota((n_src, n_dst), 0)
    dst = _iota((n_src, n_dst), 1)
    gmap = src - hc + LANES
    fbase = jnp.zeros_like(src)
    for j, base in enumerate(f_bases):
        fbase = jnp.where(src == j, base, fbase)
    hit = ((src < hc) & (dst >= fbase) & (dst < fbase + 6)) | ((src >= hc) & (dst == gmap))
    move = (hit & (src < n_in - n_main)).astype(BF16)
    o_ref[0, :, n_main:] = jnp.dot(x, move, preferred_element_type=F32).astype(BF16)


def _wprep_in(w_in, n_main, q_ranges, f_bases, tr=256):
    depth, d, n_in = w_in.shape
    n_out = n_main + LANES + (n_in - n_main - len(f_bases))
    assert n_main % LANES == 0 and n_out % LANES == 0 and d % tr == 0
    return pl.pallas_call(
        functools.partial(_wprep_in_kernel, n_main=n_main, q_ranges=q_ranges, f_bases=f_bases),
        grid=(depth, d // tr),
        in_specs=[pl.BlockSpec((1, tr, n_in), lambda l, i: (l, i, 0))],
        out_specs=pl.BlockSpec((1, tr, n_out), lambda l, i: (l, i, 0)),
        out_shape=jax.ShapeDtypeStruct((depth, d, n_out), BF16),
        scratch_shapes=[pltpu.VMEM((tr, LANES), F32)],
        compiler_params=_cparams(("parallel", "parallel")),
        name="wprep_in",
    )(w_in)


def _wprep_out_kernel(w_ref, o_ref):
    o_ref[...] = w_ref[...].astype(BF16)


def _wprep_out(w_out):
    depth, k, d = w_out.shape
    return pl.pallas_call(
        _wprep_out_kernel,
        grid=(depth,),
        in_specs=[pl.BlockSpec((1, k, d), lambda l: (l, 0, 0))],
        out_specs=pl.BlockSpec((1, k, d), lambda l: (l, 0, 0)),
        out_shape=jax.ShapeDtypeStruct((depth, k, d), BF16),
        compiler_params=_cparams(("parallel",)),
        name="wprep_out",
    )(w_out)


def _inproj_kernel(x_ref, nw_ref, w_ref, *out_refs, n_chunk):
    x = x_ref[...]
    xn = x * lax.rsqrt(jnp.mean(x * x, axis=-1, keepdims=True) + NORM_EPS)
    h = (xn * nw_ref[...]).astype(BF16)
    outs, off = [], 0
    for ref in out_refs:
        outs.append((ref, off, off + ref.shape[-1]))
        off += ref.shape[-1]
    for c0 in range(0, off, n_chunk):
        c1 = min(c0 + n_chunk, off)
        res = jnp.dot(h, w_ref[:, c0:c1], preferred_element_type=F32)
        for ref, o0, o1 in outs:
            a, b = max(c0, o0), min(c1, o1)
            if a < b:
                ref[:, a - o0:b - o0] = res[:, a - c0:b - c0].astype(ref.dtype)


def _inproj(x2, nw, w, layer, widths, dtypes, tm=512, n_chunk=512):
    m, d = x2.shape
    n = w.shape[2]
    assert sum(widths) == n and m % tm == 0
    out_shape = [jax.ShapeDtypeStruct((m, wd), dt) for wd, dt in zip(widths, dtypes)]
    out_specs = [pl.BlockSpec((tm, wd), lambda i: (i, 0)) for wd in widths]
    return pl.pallas_call(
        functools.partial(_inproj_kernel, n_chunk=n_chunk),
        grid=(m // tm,),
        in_specs=[pl.BlockSpec((tm, d), lambda i: (i, 0)),
                  pl.BlockSpec((1, d), lambda i: (0, 0)),
                  pl.BlockSpec((None, d, n), lambda i: (layer, 0, 0),
                               pipeline_mode=pl.Buffered(1))],
        out_specs=out_specs,
        out_shape=out_shape,
        compiler_params=_cparams(("parallel",)),
        name="inproj",
    )(x2, nw, w)


def _outproj_kernel(x_ref, ya_ref, yb_ref, yc_ref, g_ref, w_ref, fw_ref, o_ref, *, final):
    g = g_ref[...].astype(F32)
    nb = yb_ref.shape[-1] - LANES
    y = jnp.concatenate([ya_ref[...], yb_ref[:, 0:nb], yb_ref[:, nb:] + yc_ref[:, 0:LANES],
                         yc_ref[:, LANES:]], axis=1).astype(F32)
    yg = (y * (g * _sigmoid(g))).astype(BF16)
    acc = x_ref[...] + jnp.dot(yg, w_ref[...], preferred_element_type=F32)
    if final:
        acc = acc * lax.rsqrt(jnp.mean(acc * acc, axis=-1, keepdims=True) + NORM_EPS) * fw_ref[...]
    o_ref[...] = acc


def _outproj(x2, ya, yb, yc, g, w, layer, fw, final, tm=1024):
    m, d = x2.shape
    row = lambda wd: pl.BlockSpec((tm, wd), lambda i: (i, 0))
    return pl.pallas_call(
        functools.partial(_outproj_kernel, final=final),
        grid=(m // tm,),
        in_specs=[row(d), row(ya.shape[1]), row(yb.shape[1]), row(yc.shape[1]), row(g.shape[1]),
                  pl.BlockSpec((None,) + w.shape[1:], lambda i: (layer, 0, 0)),
                  pl.BlockSpec((1, d), lambda i: (0, 0))],
        out_specs=row(d),
        out_shape=jax.ShapeDtypeStruct((m, d), F32),
        compiler_params=_cparams(("parallel",)),
        name="outproj",
    )(x2, ya, yb, yc, g, w, fw)


def _fcum_kernel(f_ref, bf_ref, c_ref, *, blk):
    s = f_ref.shape[1]
    row = _iota((blk, blk), 0)
    col = _iota((blk, blk), 1)
    tri = (col <= row).astype(BF16)
    carry = jnp.zeros((1, LANES), F32)
    for i in range(s // blk):
        lf = _log_sigmoid(f_ref[0, i * blk:(i + 1) * blk, :] + bf_ref[...])
        c = _dot_lx(tri, lf, 3) + carry
        c_ref[0, i * blk:(i + 1) * blk, :] = c
        carry = c[blk - 1:blk, :]


def _fcum(f, bf, blk=256):
    b, s, _ = f.shape
    return pl.pallas_call(
        functools.partial(_fcum_kernel, blk=blk),
        grid=(b,),
        in_specs=[pl.BlockSpec((1, s, LANES), lambda i: (i, 0, 0)),
                  pl.BlockSpec((1, LANES), lambda i: (0, 0))],
        out_specs=pl.BlockSpec((1, s, LANES), lambda i: (i, 0, 0)),
        out_shape=jax.ShapeDtypeStruct((b, s, LANES), F32),
        compiler_params=_cparams(("parallel",)),
        name="fcum",
    )(f, bf)


class _SbBlock:
    def __init__(self, q_ref, kv_ref, o_ref, qi, rows, heads):
        t = rows.stop - rows.start
        self.q_ref, self.kv_ref, self.o_ref = q_ref, kv_ref, o_ref
        self.qi, self.rows, self.heads, self.t = qi, rows, heads, t
        self.lane = _iota((t, LANES), 1)
        row = _iota((t, t), 0)
        col = _iota((t, t), 1)
        self.strict = col < row
        self.tri = (row > col).astype(BF16)

    def kv(self, k0, hd):
        rows = pl.ds(k0, self.t)
        return self.kv_ref[0, rows, _lanes(hd.k[0])], self.kv_ref[0, rows, _lanes(hd.v[0])]

    def logits(self, qh, k2, diag):
        z = _dot_nt(qh, k2)
        ls = jnp.minimum(z, 0.0) - jnp.log(1.0 + jnp.exp(-jnp.abs(z)))
        log1m = ls - z
        if diag:
            log1m = jnp.where(self.strict, log1m, 0.0)
        return ls, _dot_xr(log1m, self.tri, 1), jnp.sum(log1m, axis=-1, keepdims=True)

    def front(self):
        t, qi = self.t, self.qi
        q0 = pl.multiple_of(qi * t, t)
        p0 = pl.multiple_of(jnp.maximum(qi - 1, 0) * t, t)
        no_prev = jnp.where(qi > 0, 0.0, -1e30).astype(F32)
        self.qhs, self.accs, self.carries = [], [], []
        for hd in self.heads:
            qh = _query_in_key_half(self.q_ref[0, self.rows, _lanes(hd.q[0])], self.lane, hd)
            kd, vd = self.kv(q0, hd)
            kp, vp = self.kv(p0, hd)
            ls_d, after_d, mass_d = self.logits(qh, kd, True)
            ls_p, after_p, mass_p = self.logits(qh, kp, False)
            attn_d = jnp.where(self.strict, jnp.exp(ls_d + after_d), 0.0)
            attn_p = jnp.exp(ls_p + after_p + (mass_d + no_prev))
            self.qhs.append(qh)
            self.accs.append(_dot(attn_d, vd) + _dot(attn_p, vp))
            self.carries.append(mass_d + mass_p)
            yield

    def finish(self):
        t = self.t

        def any_live(carries):
            m = carries[0]
            for c in carries[1:]:
                m = jnp.maximum(m, c)
            return (jnp.max(m) > EXP_ZERO).astype(jnp.int32)

        def cond(st):
            return (st[0] >= 0) & (st[1] > 0)

        def body(st):
            kb, _, accs, carries = st
            k0 = pl.multiple_of(kb * t, t)
            new_accs, new_carries = [], []
            for hd, qh, acc, carry in zip(self.heads, self.qhs, accs, carries):
                k2, v2 = self.kv(k0, hd)
                ls, after, mass = self.logits(qh, k2, False)
                new_accs.append(acc + _dot(jnp.exp(ls + after + carry), v2))
                new_carries.append(carry + mass)
            return kb - 1, any_live(new_carries), tuple(new_accs), tuple(new_carries)

        st = lax.while_loop(cond, body, (self.qi - 2, any_live(self.carries), tuple(self.accs),
                                         tuple(self.carries)))
        _write_heads(self.o_ref, self.rows, self.lane, self.heads, st[2])


BIAS_GROUP = 8


def _bias_terms(val, lane, key_side):
    hi, mid, lo = _split(val, 3)
    pos = _rem(lane, BIAS_GROUP)
    one = jnp.ones((), BF16)
    if key_side:
        return jnp.where(pos < 3, one, jnp.where(pos == 3, -hi, jnp.where(pos == 4, -mid, -lo)))
    return jnp.where(pos == 0, hi, jnp.where(pos == 1, mid, jnp.where(pos == 2, lo, one)))


def _bias_group(lane, hd):
    return (lane >= hd.c) & (lane < hd.c + 6)


def _fox_kernel(q_ref, kv_ref, cq_ref, ck_ref, o_ref, kaug_ref, vaug_ref, acc_ref, kmax_ref,
                *, t, heads):
    qi = pl.program_id(1)
    s_len = kv_ref.shape[1]
    lane = _iota((t, LANES), 1)
    causal = _iota((t, t), 1) <= _iota((t, t), 0)
    q0 = pl.multiple_of(qi * t, t)
    ones_lane = [HEAD_DIM * (1 - hd.v[1]) for hd in heads]
    for hd in heads:
        assert hd.c % BIAS_GROUP == 0 and hd.c // HEAD_DIM == 1 - hd.k[1]

    @pl.when(qi == 0)
    def _():
        lane_s = _iota((s_len, LANES), 1)
        key_terms = _bias_terms(ck_ref[0], lane_s, key_side=True)
        for i, hd in enumerate(heads):
            k2 = kv_ref[0, :, _lanes(hd.k[0])]
            kf = jnp.where(_head_lanes(lane_s, hd.k[1]), k2, jnp.zeros_like(k2)).astype(F32)
            kmax_ref[i] = jnp.max(jnp.sqrt(jnp.sum(kf * kf, axis=-1, keepdims=True)))
            kaug_ref[i] = jnp.where(_bias_group(lane_s, hd), key_terms, k2)
            vaug_ref[i] = jnp.where(lane_s == ones_lane[i], jnp.ones((), BF16),
                                    kv_ref[0, :, _lanes(hd.v[0])])

    owns, ubound, shift = [], None, jnp.zeros((t, LANES), F32)
    for i, hd in enumerate(heads):
        own = _query_in_key_half(q_ref[0, :, _lanes(hd.q[0])], lane, hd)
        qf = own.astype(F32)
        u = jnp.sqrt(jnp.sum(qf * qf, axis=-1, keepdims=True)) * kmax_ref[i]
        ubound = u if ubound is None else jnp.maximum(ubound, u)
        shift = jnp.where(_bias_group(lane, hd), u, shift)
        owns.append(own)
    query_terms = _bias_terms(cq_ref[0] - shift, lane, key_side=False)
    qas = [jnp.where(_bias_group(lane, hd), query_terms, own) for hd, own in zip(heads, owns)]
    safe = 2.0 * jnp.max(ubound) < SHIFT_SAFE

    def logits(i, k0, diag):
        s_blk = _dot_nt(qas[i], kaug_ref[i, pl.ds(k0, t), :])
        return jnp.where(causal, s_blk, -1e30) if diag else s_blk

    @pl.when(safe)
    def _():
        def sweep(k0, diag):
            for i in range(len(heads)):
                p = jnp.exp(logits(i, k0, diag))
                acc_ref[i] += _dot(p, vaug_ref[i, pl.ds(k0, t), :])

        acc_ref[...] = jnp.zeros_like(acc_ref)

        @pl.loop(0, qi)
        def _(j):
            sweep(pl.multiple_of(j * t, t), False)

        sweep(q0, True)
        outs = []
        for i in range(len(heads)):
            acc = acc_ref[i]
            l = jnp.sum(jnp.where(lane == ones_lane[i], acc, 0.0), axis=-1, keepdims=True)
            outs.append(acc * (1.0 / l))
        _write_heads(o_ref, slice(None), lane, heads, outs)

    @pl.when(jnp.logical_not(safe))
    def _():
        def update(st, s_blk, v2):
            m, l, acc = st
            m_new = jnp.maximum(m, jnp.max(s_blk, axis=-1, keepdims=True))
            alpha = jnp.exp(m - m_new)
            p = jnp.exp(s_blk - m_new)
            l = alpha * l + jnp.sum(p, axis=-1, keepdims=True)
            acc = alpha * acc + _dot(p, v2)
            return m_new, l, acc

        def sweep(k0, sts, diag):
            return tuple(update(st, logits(i, k0, diag), vaug_ref[i, pl.ds(k0, t), :])
                         for i, st in enumerate(sts))

        init = tuple((jnp.full((t, 1), -1e30, F32), jnp.zeros((t, 1), F32),
                      jnp.zeros((t, LANES), F32)) for _ in heads)
        sts = lax.fori_loop(0, qi, lambda j, sts: sweep(pl.multiple_of(j * t, t), sts, False),
                            init)
        sts = sweep(q0, sts, True)
        _write_heads(o_ref, slice(None), lane, heads, [acc * (1.0 / l) for _, l, acc in sts])


def _fox_attention(qkv, c, heads, n_out_blocks, t=512):
    b, s, w = qkv.shape
    return pl.pallas_call(
        functools.partial(_fox_kernel, t=t, heads=heads),
        grid=(b, s // t),
        in_specs=[pl.BlockSpec((1, t, w), lambda bi, qi: (bi, qi, 0)),
                  pl.BlockSpec((1, s, w), lambda bi, qi: (bi, 0, 0)),
                  pl.BlockSpec((1, t, LANES), lambda bi, qi: (bi, qi, 0)),
                  pl.BlockSpec((1, s, LANES), lambda bi, qi: (bi, 0, 0))],
        out_specs=pl.BlockSpec((1, t, n_out_blocks * LANES), lambda bi, qi: (bi, qi, 0)),
        out_shape=jax.ShapeDtypeStruct((b, s, n_out_blocks * LANES), BF16),
        scratch_shapes=[pltpu.VMEM((len(heads), s, LANES), BF16),
                        pltpu.VMEM((len(heads), s, LANES), BF16),
                        pltpu.VMEM((len(heads), t, LANES), F32),
                        pltpu.SMEM((len(heads),), F32)],
        compiler_params=_cparams(("parallel", "arbitrary")),
        name="fox",
    )(qkv, qkv, c, c)


def _unit_lower_inverses(lbds, c):
    n = lbds[0].shape[0]
    row = _iota((n, n), 0)
    col = _iota((n, n), 1)
    eye = (row == col).astype(F32)
    same2 = _blk(row, 2) == _blk(col, 2)
    xs = [eye + jnp.where(same2, l, 0.0) for l in lbds]
    s = 2
    while s < c:
        join = (_blk(row, 2 * s) == _blk(col, 2 * s)) & (_blk(row, s) != _blk(col, s))
        ts = [_dot(jnp.where(join, l, 0.0), x) for l, x in zip(lbds, xs)]
        yield True
        xs = [x + _dot(x, t) for x, t in zip(xs, ts)]
        yield True
        s *= 2
    return xs


def _rwkv_body(pa_ref, mu_ref, wl_ref, w0_ref, a0_ref, kk_ref, ka_ref, rk_ref, lnw_ref, lnb_ref,
               o_ref, prev_ref, h_ref, *, tb, c, da):
    n_pairs = da // LANES
    n_chunks = tb // c
    width = pa_ref.shape[-1]
    assert 2 * c == LANES

    p = pa_ref[0]
    prow = _iota((tb, width), 0)
    prev = jnp.where(prow == 0, prev_ref[...], pltpu.roll(p, 1, 0))
    prev_ref[...] = p[tb - 1:tb, :]
    xs = p + mu_ref[...] * (prev - p)
    r = xs[:, 0:da]
    k = xs[:, da:2 * da]
    v = xs[:, 2 * da:3 * da]
    wa = xs[:, 3 * da:3 * da + LANES]
    lane_t = _iota((tb, LANES), 1)
    wa = jnp.where(lane_t < HEAD_DIM, jnp.tanh(wa), wa)
    lora = _dot(wa, wl_ref[...])
    wpre = w0_ref[...] + lora[:, 0:da]
    w = -(jnp.maximum(-wpre, 0.0) + jnp.log1p(jnp.exp(-jnp.abs(wpre)))) - 0.5
    lw = -jnp.exp(w)
    a = _sigmoid(a0_ref[...] + lora[:, da:2 * da])
    k2 = k * (1.0 + (a - 1.0) * ka_ref[...])
    kk = k * kk_ref[...]

    er = _blk(_iota((LANES, LANES), 0), HEAD_DIM)
    ec = _blk(_iota((LANES, LANES), 1), HEAD_DIM)
    bd_mask = er == ec
    ones_head = bd_mask.astype(BF16)
    diag_eye = (_iota((LANES, LANES), 0) == _iota((LANES, LANES), 1)).astype(F32)

    pairs = [_lanes(i) for i in range(n_pairs)]
    kkn = jnp.concatenate(
        [kk[:, sl] * lax.rsqrt(jnp.maximum(_dot_xr(kk[:, sl] * kk[:, sl], ones_head, 2), 1e-24))
         for sl in pairs], axis=1)
    av = -kkn
    bv = kkn * a

    grp = min(tb, 2 * LANES)
    tr = _iota((grp, grp), 0)
    tc = _iota((grp, grp), 1)
    tri_blk = ((_blk(tr, c) == _blk(tc, c)) & (tc <= tr)).astype(BF16)
    cum = jnp.concatenate([_dot_lx(tri_blk, lw[i:i + grp], 2) for i in range(0, tb, grp)], axis=0)
    eg = jnp.exp(cum)
    rt = r * eg
    at = av * jnp.exp(cum - lw)
    ieg = jnp.exp(-cum)
    kt = k2 * ieg
    bt = bv * ieg

    rr = _iota((2 * c, 2 * c), 0)
    cc = _iota((2 * c, 2 * c), 1)
    same_blk = _blk(rr, c) == _blk(cc, c)
    strict_2c = _rem(cc, c) < _rem(rr, c)
    incl_2c = _rem(cc, c) <= _rem(rr, c)
    m0 = _iota((c, LANES), 1) < HEAD_DIM
    m0w = _rem(_iota((c, 2 * LANES), 1), LANES) < HEAD_DIM
    zeros_c = jnp.zeros((c, LANES), F32)

    items = [(sl, slice(j * c, (j + 1) * c)) for sl in pairs for j in range(n_chunks)]
    lbds, laks, prms, qa2s, vstk, gams, bkbars = [], [], [], [], [], [], []
    for sl, rs in items:
        rc, ac, kc, bc, vc = rt[rs, sl], at[rs, sl], kt[rs, sl], bt[rs, sl], v[rs, sl]
        gam = eg[rs.stop - 1:rs.stop, sl]
        bk = jnp.concatenate([bc, kc], axis=0)
        at0 = jnp.where(m0, ac, 0.0)
        at1 = jnp.where(m0, 0.0, ac)
        qa4 = jnp.concatenate([at0, at1, jnp.where(m0, rc, 0.0), jnp.where(m0, 0.0, rc)], axis=0)
        pm = _dot_nt(qa4, bk)
        xall = jnp.concatenate([pm[0:c], pltpu.roll(pm[c:2 * c], c, 1)], axis=0)
        lbds.append(jnp.where(same_blk & strict_2c, xall, 0.0))
        laks.append(jnp.where((~same_blk) & strict_2c, xall, 0.0))
        prms.append(jnp.where(incl_2c, pm[2 * c:], 0.0))
        qa2s.append(jnp.concatenate([at0, at1], axis=0))
        vstk.append(jnp.concatenate([jnp.where(m0, 0.0, vc), jnp.where(m0, vc, 0.0)], axis=0))
        gams.append(gam)
        bkbars.append(jnp.concatenate([bc * gam, kc * gam], axis=0))

    yield True
    tinvs = yield from _unit_lower_inverses(lbds, c)
    w1s = [_dot(lak, vs) for lak, vs in zip(laks, vstk)]
    tzs = [_dot(tinv, jnp.concatenate([qa2, w1], axis=1))
           for tinv, qa2, w1 in zip(tinvs, qa2s, w1s)]
    r3s, y2s, phis, gs = [], [], [], []
    for (sl, rs), tz, prm, bkbar, gam in zip(items, tzs, prms, bkbars, gams):
        a2v2 = tz[0:c] + tz[c:2 * c]
        rhs = jnp.concatenate([a2v2, jnp.concatenate([zeros_c, v[rs, sl]], axis=1)], axis=0)
        res = _dot(jnp.concatenate([prm, bkbar.T], axis=0), rhs)
        ry = jnp.where(m0w, res[0:c], res[c:2 * c])
        r3s.append(rt[rs, sl] + ry[:, 0:LANES])
        y2s.append(ry[:, LANES:])
        pg = res[2 * c:]
        phis.append(diag_eye * gam + jnp.where(bd_mask, pg[:, 0:LANES], 0.0))
        gs.append(jnp.where(bd_mask, pg[:, LANES:], 0.0))

    hs = [h_ref[i] for i in range(n_pairs)]
    ys = [[] for _ in range(n_pairs)]
    for j in range(n_chunks):
        for i in range(n_pairs):
            idx = i * n_chunks + j
            ys[i].append(_dot(r3s[idx], hs[i]) + y2s[idx])
        hs = [_dot(phis[i * n_chunks + j], hs[i]) + gs[i * n_chunks + j] for i in range(n_pairs)]
    for i, sl in enumerate(pairs):
        h_ref[i] = hs[i]
        y = jnp.concatenate(ys[i], axis=0)
        mean = _dot_xr(y, ones_head, 2) * (1.0 / HEAD_DIM)
        yc_ = y - mean
        var = _dot_xr(yc_ * yc_, ones_head, 2) * (1.0 / HEAD_DIM)
        yn = yc_ * lax.rsqrt(var + GN_EPS) * lnw_ref[:, sl] + lnb_ref[:, sl]
        bonus = _dot_xr(r[:, sl] * k2[:, sl] * rk_ref[:, sl], ones_head, 2) * v[:, sl]
        o_ref[0, :, sl] = (yn + bonus).astype(o_ref.dtype)


def _rwkv_sb_kernel(*refs, tb, c, da, t, heads):
    rwkv_in, (q_ref, kv_ref, ya_ref, yb_ref, prev_ref, h_ref) = refs[:10], refs[10:]
    ti = pl.program_id(1)

    @pl.when(ti == 0)
    def _():
        prev_ref[...] = jnp.zeros_like(prev_ref)
        h_ref[...] = jnp.zeros_like(h_ref)

    blocks = [_SbBlock(q_ref, kv_ref, yb_ref, ti * (tb // t) + j, slice(j * t, (j + 1) * t), heads)
              for j in range(tb // t)]
    sb_heads = itertools.chain(*[blk.front() for blk in blocks])
    for mxu_run_next in _rwkv_body(*rwkv_in, ya_ref, prev_ref, h_ref, tb=tb, c=c, da=da):
        if mxu_run_next:
            next(sb_heads, None)
    for _ in sb_heads:
        pass
    for blk in blocks:
        blk.finish()


def _rwkv_sb(pa, mu, wl, w0, a0, k_k, k_a, r_k, ln_w, ln_b, qkv, heads, n_out_blocks, sb_width,
             tb=512, c=64, t=256):
    b, s, width = pa.shape
    da = w0.shape[-1]
    vec = lambda n: pl.BlockSpec((1, n), lambda bi, ti: (0, 0))
    return pl.pallas_call(
        functools.partial(_rwkv_sb_kernel, tb=tb, c=c, da=da, t=t, heads=heads),
        grid=(b, s // tb),
        in_specs=[pl.BlockSpec((1, tb, width), lambda bi, ti: (bi, ti, 0)),
                  vec(width),
                  pl.BlockSpec(wl.shape, lambda bi, ti: (0, 0)),
                  vec(da), vec(da), vec(da), vec(da), vec(da), vec(da), vec(da),
                  pl.BlockSpec((1, tb, sb_width), lambda bi, ti: (bi, ti, 0)),
                  pl.BlockSpec((1, s, sb_width), lambda bi, ti: (bi, 0, 0))],
        out_specs=[pl.BlockSpec((1, tb, da), lambda bi, ti: (bi, ti, 0)),
                   pl.BlockSpec((1, tb, n_out_blocks * LANES), lambda bi, ti: (bi, ti, 0))],
        out_shape=[jax.ShapeDtypeStruct((b, s, da), BF16),
                   jax.ShapeDtypeStruct((b, s, n_out_blocks * LANES), BF16)],
        scratch_shapes=[pltpu.VMEM((1, width), F32),
                        pltpu.VMEM((da // LANES, LANES, LANES), F32)],
        compiler_params=_cparams(("parallel", "arbitrary")),
        name="rwkv7_stickbreak",
    )(pa, mu, wl, w0, a0, k_k, k_a, r_k, ln_w, ln_b, qkv, qkv)


def kernel(x, norm_w, w_in, b_f, mu, w0, w_up, a0, a_up, k_k, k_a, r_k, ln_x_w, ln_x_b, w_out,
           final_norm_w):
    b, s, d = x.shape
    depth = w_in.shape[0]
    da = w0.shape[-1]
    lora = w_up.shape[1]
    hc = b_f.shape[-1]
    d_mix = w_out.shape[1]
    db = (d_mix - da) // 2
    n_shift = 3 * da + 2 * lora
    n_qkv = 6 * db
    assert da % LANES == 0 and 2 * lora == LANES and db == hc * HEAD_DIM and hc % 2 == 1
    assert mu.shape[-1] == n_shift and (n_shift + n_qkv) % LANES == 0
    n_out_blocks = (db + HEAD_DIM) // LANES
    dbp = n_out_blocks * LANES

    n_main = n_shift + n_qkv
    q_ranges = ((n_shift, n_shift + db), (n_shift + 3 * db, n_shift + 4 * db))
    sb_heads = tuple(_Head(q=_slot(HEAD_DIM * g), k=_slot(db + HEAD_DIM * g),
                           v=_slot(2 * db + HEAD_DIM * g), out=_slot(HEAD_DIM * g))
                     for g in range(hc))
    fox_heads = tuple(
        _Head(q=_slot(3 * db + HEAD_DIM * j), k=_slot(4 * db + HEAD_DIM * j),
              v=_slot(5 * db + HEAD_DIM * j), out=_slot(HEAD_DIM * (j + 1)),
              c=HEAD_DIM * (1 - _slot(4 * db + HEAD_DIM * j)[1]) + BIAS_GROUP * j)
        for j in range(hc))
    f_bases = tuple(hd.c for hd in fox_heads)
    assert hc * BIAS_GROUP <= HEAD_DIM
    w_all = _wprep_in(w_in, n_main, q_ranges, f_bases)
    wo_all = _wprep_out(w_out)
    sb_width = -(-(3 * db) // (2 * LANES)) * 2 * LANES
    assert all(hd.out[1] == hd.v[1] for hd in sb_heads + fox_heads) and sb_width <= n_qkv

    x2 = x.reshape(b * s, d)
    zpad = lambda rows, n: jnp.zeros((rows, n), F32)
    for l in range(depth):
        widths = (n_shift, n_qkv, LANES, d_mix)
        dtypes = (F32, BF16, F32, BF16)
        pa, qkv, f, g = _inproj(x2, norm_w[l][None, :], w_all, l, widths, dtypes)

        wl_comb = jnp.concatenate(
            [jnp.concatenate([w_up[l], zpad(lora, da)], axis=1),
             jnp.concatenate([zpad(lora, da), a_up[l]], axis=1)], axis=0).astype(BF16)
        row = lambda t: t.reshape(1, -1)
        qkv3 = qkv.reshape(b, s, n_qkv)
        ya, yb = _rwkv_sb(pa.reshape(b, s, n_shift), row(mu[l]), wl_comb, row(w0[l]), row(a0[l]),
                          row(k_k[l]), row(k_a[l]), row(r_k[l]), row(ln_x_w[l]), row(ln_x_b[l]),
                          qkv3, sb_heads, n_out_blocks, sb_width)
        lane_ids = jnp.arange(LANES)
        bf_pad = jnp.zeros((LANES,), F32)
        for j, base in enumerate(f_bases):
            bf_pad = jnp.where((lane_ids >= base) & (lane_ids < base + 6), b_f[l, j], bf_pad)
        cum_f = _fcum(f.reshape(b, s, LANES), bf_pad[None, :])
        yc = _fox_attention(qkv3, cum_f, fox_heads, n_out_blocks)

        x2 = _outproj(x2, ya.reshape(b * s, da), yb.reshape(b * s, dbp), yc.reshape(b * s, dbp),
                      g, wo_all, l, final_norm_w[None, :], final=(l == depth - 1))
    return x2.reshape(b, s, d)
```

```python
import functools
import itertools
from typing import NamedTuple

import jax
import jax.numpy as jnp
from jax import lax
from jax.experimental import pallas as pl
from jax.experimental.pallas import tpu as pltpu

HEAD_DIM = 64
LANES = 128
NORM_EPS = 1e-6
GN_EPS = 64e-5
VMEM_LIMIT = 56 * 1024 * 1024
EXP_ZERO = -104.0
SHIFT_SAFE = 80.0
NORM_SLACK = 1.01

F32 = jnp.float32
BF16 = jnp.bfloat16


class _Head(NamedTuple):
    q: tuple
    k: tuple
    v: tuple
    out: tuple
    c: int = -1


def _slot(col):
    assert col % HEAD_DIM == 0
    return (col // LANES, (col % LANES) // HEAD_DIM)


def _iota(shape, dim):
    return lax.broadcasted_iota(jnp.int32, shape, dim)


def _blk(idx, size):
    assert size & (size - 1) == 0
    return jnp.right_shift(idx, size.bit_length() - 1)


def _rem(idx, size):
    assert size & (size - 1) == 0
    return jnp.bitwise_and(idx, size - 1)


def _lanes(blk):
    return slice(blk * LANES, (blk + 1) * LANES)


def _dot(a, b):
    return jnp.dot(a.astype(BF16), b.astype(BF16), preferred_element_type=F32)


def _dot_nt(a, b):
    return lax.dot_general(a.astype(BF16), b.astype(BF16), (((1,), (1,)), ((), ())),
                           preferred_element_type=F32)


def _split(x, n):
    parts = []
    rem = x
    for i in range(n):
        p = rem.astype(BF16)
        parts.append(p)
        if i + 1 < n:
            rem = rem - p.astype(F32)
    return parts


def _dot_xr(x, r, n):
    out = None
    for p in _split(x, n):
        t = jnp.dot(p, r, preferred_element_type=F32)
        out = t if out is None else out + t
    return out


def _dot_lx(l, x, n):
    out = None
    for p in _split(x, n):
        t = jnp.dot(l, p, preferred_element_type=F32)
        out = t if out is None else out + t
    return out


def _log_sigmoid(z):
    return jnp.minimum(z, 0.0) - jnp.log1p(jnp.exp(-jnp.abs(z)))


def _sigmoid(z):
    return 1.0 / (1.0 + jnp.exp(-z))


def _head_lanes(lane, h):
    return (lane >= h * HEAD_DIM) & (lane < (h + 1) * HEAD_DIM)


def _query_in_key_half(q2, lane, hd):
    own = jnp.where(_head_lanes(lane, hd.q[1]), q2, jnp.zeros_like(q2))
    if hd.q[1] != hd.k[1]:
        own = pltpu.roll(own.astype(F32), HEAD_DIM, 1).astype(BF16)
    return own


def _write_heads(o_ref, rows, lane, heads, outs):
    t = lane.shape[0]
    for blk in sorted({hd.out[0] for hd in heads}):
        halves = {}
        for hd, o in zip(heads, outs):
            assert hd.out[1] == hd.v[1]
            if hd.out[0] == blk:
                halves[hd.out[1]] = o
        zero = jnp.zeros((t, LANES), F32)
        o_ref[0, rows, _lanes(blk)] = jnp.where(
            lane < HEAD_DIM, halves.get(0, zero), halves.get(1, zero)).astype(o_ref.dtype)


def _cparams(sem):
    return pltpu.CompilerParams(dimension_semantics=sem, vmem_limit_bytes=VMEM_LIMIT)


def _wprep_in_kernel(w_ref, o_ref, tail_ref, *, n_main, q_ranges, f_bases):
    hc = len(f_bases)
    n_in = w_ref.shape[-1]
    n_full = (n_in - n_main) // LANES * LANES
    n_tail = n_in - n_main - n_full
    n_src = n_full + LANES
    n_dst = o_ref.shape[-1] - n_main
    col = _iota((1, n_main), 1)
    is_q = None
    for a, b in q_ranges:
        m = (col >= a) & (col < b)
        is_q = m if is_q is None else (is_q | m)
    scale = jnp.where(is_q, HEAD_DIM ** -0.5, 1.0).astype(F32)
    o_ref[0, :, 0:n_main] = (w_ref[0, :, 0:n_main] * scale).astype(BF16)
    tail_ref[...] = jnp.zeros_like(tail_ref)
    tail_ref[:, 0:n_tail] = w_ref[0, :, n_main + n_full:n_in]
    x = jnp.concatenate([w_ref[0, :, n_main:n_main + n_full], tail_ref[...]], axis=1).astype(BF16)
    src = _iota((n_src, n_dst), 0)
    dst = _iota((n_src, n_dst), 1)
    gmap = src - hc + LANES
    fbase = jnp.zeros_like(src)
    for j, base in enumerate(f_bases):
        fbase = jnp.where(src == j, base, fbase)
    hit = ((src < hc) & (dst >= fbase) & (dst < fbase + 6)) | ((src >= hc) & (dst == gmap))
    move = (hit & (src < n_in - n_main)).astype(BF16)
    o_ref[0, :, n_main:] = jnp.dot(x, move, preferred_element_type=F32).astype(BF16)


def _wprep_in(w_in, n_main, q_ranges, f_bases, tr=256):
    depth, d, n_in = w_in.shape
    n_out = n_main + LANES + (n_in - n_main - len(f_bases))
    assert n_main % LANES == 0 and n_out % LANES == 0 and d % tr == 0
    return pl.pallas_call(
        functools.partial(_wprep_in_kernel, n_main=n_main, q_ranges=q_ranges, f_bases=f_bases),
        grid=(depth, d // tr),
        in_specs=[pl.BlockSpec((1, tr, n_in), lambda l, i: (l, i, 0))],
        out_specs=pl.BlockSpec((1, tr, n_out), lambda l, i: (l, i, 0)),
        out_shape=jax.ShapeDtypeStruct((depth, d, n_out), BF16),
        scratch_shapes=[pltpu.VMEM((tr, LANES), F32)],
        compiler_params=_cparams(("parallel", "parallel")),
        name="wprep_in",
    )(w_in)


def _wprep_out_kernel(w_ref, o_ref):
    o_ref[...] = w_ref[...].astype(BF16)


def _wprep_out(w_out):
    depth, k, d = w_out.shape
    return pl.pallas_call(
        _wprep_out_kernel,
        grid=(depth,),
        in_specs=[pl.BlockSpec((1, k, d), lambda l: (l, 0, 0))],
        out_specs=pl.BlockSpec((1, k, d), lambda l: (l, 0, 0)),
        out_shape=jax.ShapeDtypeStruct((depth, k, d), BF16),
        compiler_params=_cparams(("parallel",)),
        name="wprep_out",
    )(w_out)


def _inproj_kernel(x_ref, nw_ref, w_ref, *out_refs, n_chunk):
    x = x_ref[...]
    xn = x * lax.rsqrt(jnp.mean(x * x, axis=-1, keepdims=True) + NORM_EPS)
    h = (xn * nw_ref[...]).astype(BF16)
    outs, off = [], 0
    for ref in out_refs:
        outs.append((ref, off, off + ref.shape[-1]))
        off += ref.shape[-1]
    for c0 in range(0, off, n_chunk):
        c1 = min(c0 + n_chunk, off)
        res = jnp.dot(h, w_ref[:, c0:c1], preferred_element_type=F32)
        for ref, o0, o1 in outs:
            a, b = max(c0, o0), min(c1, o1)
            if a < b:
                ref[:, a - o0:b - o0] = res[:, a - c0:b - c0].astype(ref.dtype)


def _inproj(x2, nw, w, layer, widths, dtypes, tm=512, n_chunk=512):
    m, d = x2.shape
    n = w.shape[2]
    assert sum(widths) == n and m % tm == 0
    out_shape = [jax.ShapeDtypeStruct((m, wd), dt) for wd, dt in zip(widths, dtypes)]
    out_specs = [pl.BlockSpec((tm, wd), lambda i: (i, 0)) for wd in widths]
    return pl.pallas_call(
        functools.partial(_inproj_kernel, n_chunk=n_chunk),
        grid=(m // tm,),
        in_specs=[pl.BlockSpec((tm, d), lambda i: (i, 0)),
                  pl.BlockSpec((1, d), lambda i: (0, 0)),
                  pl.BlockSpec((None, d, n), lambda i: (layer, 0, 0),
                               pipeline_mode=pl.Buffered(1))],
        out_specs=out_specs,
        out_shape=out_shape,
        compiler_params=_cparams(("parallel",)),
        name="inproj",
    )(x2, nw, w)


def _outproj_kernel(x_ref, ya_ref, yb_ref, yc_ref, g_ref, w_ref, fw_ref, o_ref, *, final):
    g = g_ref[...].astype(F32)
    nb = yb_ref.shape[-1] - LANES
    y = jnp.concatenate([ya_ref[...], yb_ref[:, 0:nb], yb_ref[:, nb:] + yc_ref[:, 0:LANES],
                         yc_ref[:, LANES:]], axis=1).astype(F32)
    yg = (y * (g * _sigmoid(g))).astype(BF16)
    acc = x_ref[...] + jnp.dot(yg, w_ref[...], preferred_element_type=F32)
    if final:
        acc = acc * lax.rsqrt(jnp.mean(acc * acc, axis=-1, keepdims=True) + NORM_EPS) * fw_ref[...]
    o_ref[...] = acc


def _outproj(x2, ya, yb, yc, g, w, layer, fw, final, tm=1024):
    m, d = x2.shape
    row = lambda wd: pl.BlockSpec((tm, wd), lambda i: (i, 0))
    return pl.pallas_call(
        functools.partial(_outproj_kernel, final=final),
        grid=(m // tm,),
        in_specs=[row(d), row(ya.shape[1]), row(yb.shape[1]), row(yc.shape[1]), row(g.shape[1]),
                  pl.BlockSpec((None,) + w.shape[1:], lambda i: (layer, 0, 0)),
                  pl.BlockSpec((1, d), lambda i: (0, 0))],
        out_specs=row(d),
        out_shape=jax.ShapeDtypeStruct((m, d), F32),
        compiler_params=_cparams(("parallel",)),
        name="outproj",
    )(x2, ya, yb, yc, g, w, fw)


def _fcum_kernel(f_ref, bf_ref, c_ref, *, blk):
    s = f_ref.shape[1]
    row = _iota((blk, blk), 0)
    col = _iota((blk, blk), 1)
    tri = (col <= row).astype(BF16)
    carry = jnp.zeros((1, LANES), F32)
    for i in range(s // blk):
        lf = _log_sigmoid(f_ref[0, i * blk:(i + 1) * blk, :] + bf_ref[...])
        c = _dot_lx(tri, lf, 3) + carry
        c_ref[0, i * blk:(i + 1) * blk, :] = c
        carry = c[blk - 1:blk, :]


def _fcum(f, bf, blk=256):
    b, s, _ = f.shape
    return pl.pallas_call(
        functools.partial(_fcum_kernel, blk=blk),
        grid=(b,),
        in_specs=[pl.BlockSpec((1, s, LANES), lambda i: (i, 0, 0)),
                  pl.BlockSpec((1, LANES), lambda i: (0, 0))],
        out_specs=pl.BlockSpec((1, s, LANES), lambda i: (i, 0, 0)),
        out_shape=jax.ShapeDtypeStruct((b, s, LANES), F32),
        compiler_params=_cparams(("parallel",)),
        name="fcum",
    )(f, bf)


class _SbBlock:
    def __init__(self, q_ref, kv_ref, o_ref, qi, rows, heads):
        t = rows.stop - rows.start
        self.q_ref, self.kv_ref, self.o_ref = q_ref, kv_ref, o_ref
        self.qi, self.rows, self.heads, self.t = qi, rows, heads, t
        self.lane = _iota((t, LANES), 1)
        row = _iota((t, t), 0)
        col = _iota((t, t), 1)
        self.strict = col < row
        self.tri = (row > col).astype(BF16)

    def kv(self, k0, hd):
        rows = pl.ds(k0, self.t)
        return self.kv_ref[0, rows, _lanes(hd.k[0])], self.kv_ref[0, rows, _lanes(hd.v[0])]

    def logits(self, qh, k2, diag):
        z = _dot_nt(qh, k2)
        ls = jnp.minimum(z, 0.0) - jnp.log(1.0 + jnp.exp(-jnp.abs(z)))
        log1m = ls - z
        if diag:
            log1m = jnp.where(self.strict, log1m, 0.0)
        return ls, _dot_xr(log1m, self.tri, 1), jnp.sum(log1m, axis=-1, keepdims=True)

    def front(self):
        t, qi = self.t, self.qi
        q0 = pl.multiple_of(qi * t, t)
        p0 = pl.multiple_of(jnp.maximum(qi - 1, 0) * t, t)
        no_prev = jnp.where(qi > 0, 0.0, -1e30).astype(F32)
        self.qhs, self.accs, self.carries = [], [], []
        for hd in self.heads:
            qh = _query_in_key_half(self.q_ref[0, self.rows, _lanes(hd.q[0])], self.lane, hd)
            kd, vd = self.kv(q0, hd)
            kp, vp = self.kv(p0, hd)
            ls_d, after_d, mass_d = self.logits(qh, kd, True)
            ls_p, after_p, mass_p = self.logits(qh, kp, False)
            attn_d = jnp.where(self.strict, jnp.exp(ls_d + after_d), 0.0)
            attn_p = jnp.exp(ls_p + after_p + (mass_d + no_prev))
            self.qhs.append(qh)
            self.accs.append(_dot(attn_d, vd) + _dot(attn_p, vp))
            self.carries.append(mass_d + mass_p)
            yield

    def finish(self):
        t = self.t

        def any_live(carries):
            m = carries[0]
            for c in carries[1:]:
                m = jnp.maximum(m, c)
            return (jnp.max(m) > EXP_ZERO).astype(jnp.int32)

        def cond(st):
            return (st[0] >= 0) & (st[1] > 0)

        def body(st):
            kb, _, accs, carries = st
            k0 = pl.multiple_of(kb * t, t)
            new_accs, new_carries = [], []
            for hd, qh, acc, carry in zip(self.heads, self.qhs, accs, carries):
                k2, v2 = self.kv(k0, hd)
                ls, after, mass = self.logits(qh, k2, False)
                new_accs.append(acc + _dot(jnp.exp(ls + after + carry), v2))
                new_carries.append(carry + mass)
            return kb - 1, any_live(new_carries), tuple(new_accs), tuple(new_carries)

        st = lax.while_loop(cond, body, (self.qi - 2, any_live(self.carries), tuple(self.accs),
                                         tuple(self.carries)))
        _write_heads(self.o_ref, self.rows, self.lane, self.heads, st[2])


BIAS_GROUP = 8


def _bias_terms(val, lane, key_side):
    hi, mid, lo = _split(val, 3)
    pos = _rem(lane, BIAS_GROUP)
    one = jnp.ones((), BF16)
    if key_side:
        return jnp.where(pos < 3, one, jnp.where(pos == 3, -hi, jnp.where(pos == 4, -mid, -lo)))
    return jnp.where(pos == 0, hi, jnp.where(pos == 1, mid, jnp.where(pos == 2, lo, one)))


def _bias_group(lane, hd):
    return (lane >= hd.c) & (lane < hd.c + 6)


def _fox_kernel(q_ref, kv_ref, cq_ref, ck_ref, o_ref, kaug_ref, vaug_ref, acc_ref, kmax_ref,
                *, t, heads):
    qi = pl.program_id(1)
    s_len = kv_ref.shape[1]
    lane = _iota((t, LANES), 1)
    causal = _iota((t, t), 1) <= _iota((t, t), 0)
    q0 = pl.multiple_of(qi * t, t)
    ones_lane = [HEAD_DIM * (1 - hd.v[1]) for hd in heads]
    for hd in heads:
        assert hd.c % BIAS_GROUP == 0 and hd.c // HEAD_DIM == 1 - hd.k[1]

    half_of = _blk(_iota((LANES, LANES), 0), HEAD_DIM) == _blk(_iota((LANES, LANES), 1), HEAD_DIM)
    ones_half = half_of.astype(BF16)
    ones_all = jnp.ones((LANES, LANES), BF16)

    @pl.when(qi == 0)
    def _():
        lane_s = _iota((s_len, LANES), 1)
        lane_1 = _iota((1, LANES), 1)
        key_terms = _bias_terms(ck_ref[0], lane_s, key_side=True)
        norms = {}
        for i, hd in enumerate(heads):
            k2 = kv_ref[0, :, _lanes(hd.k[0])]
            if hd.k[0] not in norms:
                kf = k2.astype(F32)
                norms[hd.k[0]] = jnp.sqrt(jnp.max(_dot(kf * kf, ones_half), axis=0, keepdims=True))
            kmax_ref[i] = jnp.max(jnp.where(_head_lanes(lane_1, hd.k[1]),
                                            norms[hd.k[0]] * (NORM_SLACK * NORM_SLACK), 0.0))
            kaug_ref[i] = jnp.where(_bias_group(lane_s, hd), key_terms, k2)
            vaug_ref[i] = jnp.where(lane_s == ones_lane[i], jnp.ones((), BF16),
                                    kv_ref[0, :, _lanes(hd.v[0])])

    owns, ubound, shift = [], None, jnp.zeros((t, LANES), F32)
    for i, hd in enumerate(heads):
        own = _query_in_key_half(q_ref[0, :, _lanes(hd.q[0])], lane, hd)
        qf = own.astype(F32)
        u = jnp.sqrt(_dot(qf * qf, ones_all)) * kmax_ref[i]
        ubound = u if ubound is None else jnp.maximum(ubound, u)
        shift = jnp.where(_bias_group(lane, hd), u, shift)
        owns.append(own)
    query_terms = _bias_terms(cq_ref[0] - shift, lane, key_side=False)
    qas = [jnp.where(_bias_group(lane, hd), query_terms, own) for hd, own in zip(heads, owns)]
    safe = 2.0 * jnp.max(ubound) < SHIFT_SAFE

    def logits(i, k0, diag):
        s_blk = _dot_nt(qas[i], kaug_ref[i, pl.ds(k0, t), :])
        return jnp.where(causal, s_blk, -1e30) if diag else s_blk

    @pl.when(safe)
    def _():
        def sweep(k0, diag):
            for i in range(len(heads)):
                p = jnp.exp(logits(i, k0, diag))
                acc_ref[i] += _dot(p, vaug_ref[i, pl.ds(k0, t), :])

        acc_ref[...] = jnp.zeros_like(acc_ref)

        @pl.loop(0, qi)
        def _(j):
            sweep(pl.multiple_of(j * t, t), False)

        sweep(q0, True)
        outs = []
        for i in range(len(heads)):
            acc = acc_ref[i]
            l = jnp.sum(jnp.where(lane == ones_lane[i], acc, 0.0), axis=-1, keepdims=True)
            outs.append(acc * (1.0 / l))
        _write_heads(o_ref, slice(None), lane, heads, outs)

    @pl.when(jnp.logical_not(safe))
    def _():
        def update(st, s_blk, v2):
            m, l, acc = st
            m_new = jnp.maximum(m, jnp.max(s_blk, axis=-1, keepdims=True))
            alpha = jnp.exp(m - m_new)
            p = jnp.exp(s_blk - m_new)
            l = alpha * l + jnp.sum(p, axis=-1, keepdims=True)
            acc = alpha * acc + _dot(p, v2)
            return m_new, l, acc

        def sweep(k0, sts, diag):
            return tuple(update(st, logits(i, k0, diag), vaug_ref[i, pl.ds(k0, t), :])
                         for i, st in enumerate(sts))

        init = tuple((jnp.full((t, 1), -1e30, F32), jnp.zeros((t, 1), F32),
                      jnp.zeros((t, LANES), F32)) for _ in heads)
        sts = lax.fori_loop(0, qi, lambda j, sts: sweep(pl.multiple_of(j * t, t), sts, False),
                            init)
        sts = sweep(q0, sts, True)
        _write_heads(o_ref, slice(None), lane, heads, [acc * (1.0 / l) for _, l, acc in sts])


def _fox_attention(qkv, c, heads, n_out_blocks, t=512):
    b, s, w = qkv.shape
    return pl.pallas_call(
        functools.partial(_fox_kernel, t=t, heads=heads),
        grid=(b, s // t),
        in_specs=[pl.BlockSpec((1, t, w), lambda bi, qi: (bi, qi, 0)),
                  pl.BlockSpec((1, s, w), lambda bi, qi: (bi, 0, 0)),
                  pl.BlockSpec((1, t, LANES), lambda bi, qi: (bi, qi, 0)),
                  pl.BlockSpec((1, s, LANES), lambda bi, qi: (bi, 0, 0))],
        out_specs=pl.BlockSpec((1, t, n_out_blocks * LANES), lambda bi, qi: (bi, qi, 0)),
        out_shape=jax.ShapeDtypeStruct((b, s, n_out_blocks * LANES), BF16),
        scratch_shapes=[pltpu.VMEM((len(heads), s, LANES), BF16),
                        pltpu.VMEM((len(heads), s, LANES), BF16),
                        pltpu.VMEM((len(heads), t, LANES), F32),
                        pltpu.SMEM((len(heads),), F32)],
        compiler_params=_cparams(("parallel", "arbitrary")),
        name="fox",
    )(qkv, qkv, c, c)


def _unit_lower_inverses(lbds, c):
    n = lbds[0].shape[0]
    row = _iota((n, n), 0)
    col = _iota((n, n), 1)
    eye = (row == col).astype(F32)
    same2 = _blk(row, 2) == _blk(col, 2)
    xs = [eye + jnp.where(same2, l, 0.0) for l in lbds]
    s = 2
    while s < c:
        join = (_blk(row, 2 * s) == _blk(col, 2 * s)) & (_blk(row, s) != _blk(col, s))
        ts = [_dot(jnp.where(join, l, 0.0), x) for l, x in zip(lbds, xs)]
        yield True
        xs = [x + _dot(x, t) for x, t in zip(xs, ts)]
        yield True
        s *= 2
    return xs


def _rwkv_body(pa_ref, mu_ref, wl_ref, w0_ref, a0_ref, kk_ref, ka_ref, rk_ref, lnw_ref, lnb_ref,
               o_ref, prev_ref, h_ref, *, tb, c, da):
    n_pairs = da // LANES
    n_chunks = tb // c
    width = pa_ref.shape[-1]
    assert 2 * c == LANES

    p = pa_ref[0]
    prow = _iota((tb, width), 0)
    prev = jnp.where(prow == 0, prev_ref[...], pltpu.roll(p, 1, 0))
    prev_ref[...] = p[tb - 1:tb, :]
    xs = p + mu_ref[...] * (prev - p)
    r = xs[:, 0:da]
    k = xs[:, da:2 * da]
    v = xs[:, 2 * da:3 * da]
    wa = xs[:, 3 * da:3 * da + LANES]
    lane_t = _iota((tb, LANES), 1)
    wa = jnp.where(lane_t < HEAD_DIM, jnp.tanh(wa), wa)
    lora = _dot(wa, wl_ref[...])
    wpre = w0_ref[...] + lora[:, 0:da]
    w = -(jnp.maximum(-wpre, 0.0) + jnp.log(1.0 + jnp.exp(-jnp.abs(wpre)))) - 0.5
    lw = -jnp.exp(w)
    a = _sigmoid(a0_ref[...] + lora[:, da:2 * da])
    k2 = k * (1.0 + (a - 1.0) * ka_ref[...])
    kk = k * kk_ref[...]

    er = _blk(_iota((LANES, LANES), 0), HEAD_DIM)
    ec = _blk(_iota((LANES, LANES), 1), HEAD_DIM)
    bd_mask = er == ec
    ones_head = bd_mask.astype(BF16)
    diag_eye = (_iota((LANES, LANES), 0) == _iota((LANES, LANES), 1)).astype(F32)

    pairs = [_lanes(i) for i in range(n_pairs)]
    kkn = jnp.concatenate(
        [kk[:, sl] * lax.rsqrt(jnp.maximum(_dot_xr(kk[:, sl] * kk[:, sl], ones_head, 1), 1e-24))
         for sl in pairs], axis=1)
    av = -kkn
    bv = kkn * a

    grp = min(tb, 2 * LANES)
    tr = _iota((grp, grp), 0)
    tc = _iota((grp, grp), 1)
    tri_blk = ((_blk(tr, c) == _blk(tc, c)) & (tc <= tr)).astype(BF16)
    cum = jnp.concatenate([_dot_lx(tri_blk, lw[i:i + grp], 2) for i in range(0, tb, grp)], axis=0)
    eg = jnp.exp(cum)
    rt = r * eg
    at = av * jnp.exp(cum - lw)
    ieg = jnp.exp(-cum)
    kt = k2 * ieg
    bt = bv * ieg

    rr = _iota((2 * c, 2 * c), 0)
    cc = _iota((2 * c, 2 * c), 1)
    same_blk = _blk(rr, c) == _blk(cc, c)
    strict_2c = _rem(cc, c) < _rem(rr, c)
    incl_2c = _rem(cc, c) <= _rem(rr, c)
    m0 = _iota((c, LANES), 1) < HEAD_DIM
    m0w = _rem(_iota((c, 2 * LANES), 1), LANES) < HEAD_DIM
    zeros_c = jnp.zeros((c, LANES), F32)

    items = [(sl, slice(j * c, (j + 1) * c)) for sl in pairs for j in range(n_chunks)]
    lbds, laks, prms, qa2s, vstk, gams, bkbars = [], [], [], [], [], [], []
    for sl, rs in items:
        rc, ac, kc, bc, vc = rt[rs, sl], at[rs, sl], kt[rs, sl], bt[rs, sl], v[rs, sl]
        gam = eg[rs.stop - 1:rs.stop, sl]
        bk = jnp.concatenate([bc, kc], axis=0)
        at0 = jnp.where(m0, ac, 0.0)
        at1 = jnp.where(m0, 0.0, ac)
        qa4 = jnp.concatenate([at0, at1, jnp.where(m0, rc, 0.0), jnp.where(m0, 0.0, rc)], axis=0)
        pm = _dot_nt(qa4, bk)
        xall = jnp.concatenate([pm[0:c], pltpu.roll(pm[c:2 * c], c, 1)], axis=0)
        lbds.append(jnp.where(same_blk & strict_2c, xall, 0.0))
        laks.append(jnp.where((~same_blk) & strict_2c, xall, 0.0))
        prms.append(jnp.where(incl_2c, pm[2 * c:], 0.0))
        qa2s.append(jnp.concatenate([at0, at1], axis=0))
        vstk.append(jnp.concatenate([jnp.where(m0, 0.0, vc), jnp.where(m0, vc, 0.0)], axis=0))
        gams.append(gam)
        bkbars.append(jnp.concatenate([bc * gam, kc * gam], axis=0))

    yield True
    tinvs = yield from _unit_lower_inverses(lbds, c)
    w1s = [_dot(lak, vs) for lak, vs in zip(laks, vstk)]
    tzs = [_dot(tinv, jnp.concatenate([qa2, w1], axis=1))
           for tinv, qa2, w1 in zip(tinvs, qa2s, w1s)]
    r3s, y2s, phis, gs = [], [], [], []
    for (sl, rs), tz, prm, bkbar, gam in zip(items, tzs, prms, bkbars, gams):
        a2v2 = tz[0:c] + tz[c:2 * c]
        rhs = jnp.concatenate([a2v2, jnp.concatenate([zeros_c, v[rs, sl]], axis=1)], axis=0)
        res = _dot(jnp.concatenate([prm, bkbar.T], axis=0), rhs)
        ry = jnp.where(m0w, res[0:c], res[c:2 * c])
        r3s.append(rt[rs, sl] + ry[:, 0:LANES])
        y2s.append(ry[:, LANES:])
        pg = res[2 * c:]
        phis.append(diag_eye * gam + jnp.where(bd_mask, pg[:, 0:LANES], 0.0))
        gs.append(jnp.where(bd_mask, pg[:, LANES:], 0.0))

    hs = [h_ref[i] for i in range(n_pairs)]
    ys = [[] for _ in range(n_pairs)]
    for j in range(n_chunks):
        for i in range(n_pairs):
            idx = i * n_chunks + j
            ys[i].append(_dot(r3s[idx], hs[i]) + y2s[idx])
        hs = [_dot(phis[i * n_chunks + j], hs[i]) + gs[i * n_chunks + j] for i in range(n_pairs)]
    for i, sl in enumerate(pairs):
        h_ref[i] = hs[i]
        y = jnp.concatenate(ys[i], axis=0)
        mean = _dot_xr(y, ones_head, 2) * (1.0 / HEAD_DIM)
        yc_ = y - mean
        var = _dot_xr(yc_ * yc_, ones_head, 1) * (1.0 / HEAD_DIM)
        yn = yc_ * lax.rsqrt(var + GN_EPS) * lnw_ref[:, sl] + lnb_ref[:, sl]
        bonus = _dot_xr(r[:, sl] * k2[:, sl] * rk_ref[:, sl], ones_head, 1) * v[:, sl]
        o_ref[0, :, sl] = (yn + bonus).astype(o_ref.dtype)


def _rwkv_sb_kernel(*refs, tb, c, da, t, heads):
    rwkv_in, (q_ref, kv_ref, ya_ref, yb_ref, prev_ref, h_ref) = refs[:10], refs[10:]
    ti = pl.program_id(1)

    @pl.when(ti == 0)
    def _():
        prev_ref[...] = jnp.zeros_like(prev_ref)
        h_ref[...] = jnp.zeros_like(h_ref)

    blocks = [_SbBlock(q_ref, kv_ref, yb_ref, ti * (tb // t) + j, slice(j * t, (j + 1) * t), heads)
              for j in range(tb // t)]
    sb_heads = itertools.chain(*[blk.front() for blk in blocks])
    for mxu_run_next in _rwkv_body(*rwkv_in, ya_ref, prev_ref, h_ref, tb=tb, c=c, da=da):
        if mxu_run_next:
            next(sb_heads, None)
    for _ in sb_heads:
        pass
    for blk in blocks:
        blk.finish()


def _rwkv_sb(pa, mu, wl, w0, a0, k_k, k_a, r_k, ln_w, ln_b, qkv, heads, n_out_blocks, sb_width,
             tb=512, c=64, t=256):
    b, s, width = pa.shape
    da = w0.shape[-1]
    vec = lambda n: pl.BlockSpec((1, n), lambda bi, ti: (0, 0))
    return pl.pallas_call(
        functools.partial(_rwkv_sb_kernel, tb=tb, c=c, da=da, t=t, heads=heads),
        grid=(b, s // tb),
        in_specs=[pl.BlockSpec((1, tb, width), lambda bi, ti: (bi, ti, 0)),
                  vec(width),
                  pl.BlockSpec(wl.shape, lambda bi, ti: (0, 0)),
                  vec(da), vec(da), vec(da), vec(da), vec(da), vec(da), vec(da),
                  pl.BlockSpec((1, tb, sb_width), lambda bi, ti: (bi, ti, 0)),
                  pl.BlockSpec((1, s, sb_width), lambda bi, ti: (bi, 0, 0))],
        out_specs=[pl.BlockSpec((1, tb, da), lambda bi, ti: (bi, ti, 0)),
                   pl.BlockSpec((1, tb, n_out_blocks * LANES), lambda bi, ti: (bi, ti, 0))],
        out_shape=[jax.ShapeDtypeStruct((b, s, da), BF16),
                   jax.ShapeDtypeStruct((b, s, n_out_blocks * LANES), BF16)],
        scratch_shapes=[pltpu.VMEM((1, width), F32),
                        pltpu.VMEM((da // LANES, LANES, LANES), F32)],
        compiler_params=_cparams(("parallel", "arbitrary")),
        name="rwkv7_stickbreak",
    )(pa, mu, wl, w0, a0, k_k, k_a, r_k, ln_w, ln_b, qkv, qkv)


def kernel(x, norm_w, w_in, b_f, mu, w0, w_up, a0, a_up, k_k, k_a, r_k, ln_x_w, ln_x_b, w_out,
           final_norm_w):
    b, s, d = x.shape
    depth = w_in.shape[0]
    da = w0.shape[-1]
    lora = w_up.shape[1]
    hc = b_f.shape[-1]
    d_mix = w_out.shape[1]
    db = (d_mix - da) // 2
    n_shift = 3 * da + 2 * lora
    n_qkv = 6 * db
    assert da % LANES == 0 and 2 * lora == LANES and db == hc * HEAD_DIM and hc % 2 == 1
    assert mu.shape[-1] == n_shift and (n_shift + n_qkv) % LANES == 0
    n_out_blocks = (db + HEAD_DIM) // LANES
    dbp = n_out_blocks * LANES

    n_main = n_shift + n_qkv
    q_ranges = ((n_shift, n_shift + db), (n_shift + 3 * db, n_shift + 4 * db))
    sb_heads = tuple(_Head(q=_slot(HEAD_DIM * g), k=_slot(db + HEAD_DIM * g),
                           v=_slot(2 * db + HEAD_DIM * g), out=_slot(HEAD_DIM * g))
                     for g in range(hc))
    fox_heads = tuple(
        _Head(q=_slot(3 * db + HEAD_DIM * j), k=_slot(4 * db + HEAD_DIM * j),
              v=_slot(5 * db + HEAD_DIM * j), out=_slot(HEAD_DIM * (j + 1)),
              c=HEAD_DIM * (1 - _slot(4 * db + HEAD_DIM * j)[1]) + BIAS_GROUP * j)
        for j in range(hc))
    f_bases = tuple(hd.c for hd in fox_heads)
    assert hc * BIAS_GROUP <= HEAD_DIM
    w_all = _wprep_in(w_in, n_main, q_ranges, f_bases)
    wo_all = _wprep_out(w_out)
    sb_width = -(-(3 * db) // (2 * LANES)) * 2 * LANES
    assert all(hd.out[1] == hd.v[1] for hd in sb_heads + fox_heads) and sb_width <= n_qkv

    x2 = x.reshape(b * s, d)
    zpad = lambda rows, n: jnp.zeros((rows, n), F32)
    for l in range(depth):
        widths = (n_shift, n_qkv, LANES, d_mix)
        dtypes = (F32, BF16, F32, BF16)
        pa, qkv, f, g = _inproj(x2, norm_w[l][None, :], w_all, l, widths, dtypes)

        wl_comb = jnp.concatenate(
            [jnp.concatenate([w_up[l], zpad(lora, da)], axis=1),
             jnp.concatenate([zpad(lora, da), a_up[l]], axis=1)], axis=0).astype(BF16)
        row = lambda t: t.reshape(1, -1)
        qkv3 = qkv.reshape(b, s, n_qkv)
        ya, yb = _rwkv_sb(pa.reshape(b, s, n_shift), row(mu[l]), wl_comb, row(w0[l]), row(a0[l]),
                          row(k_k[l]), row(k_a[l]), row(r_k[l]), row(ln_x_w[l]), row(ln_x_b[l]),
                          qkv3, sb_heads, n_out_blocks, sb_width)
        lane_ids = jnp.arange(LANES)
        bf_pad = jnp.zeros((LANES,), F32)
        for j, base in enumerate(f_bases):
            bf_pad = jnp.where((lane_ids >= base) & (lane_ids < base + 6), b_f[l, j], bf_pad)
        cum_f = _fcum(f.reshape(b, s, LANES), bf_pad[None, :])
        yc = _fox_attention(qkv3, cum_f, fox_heads, n_out_blocks)

        x2 = _outproj(x2, ya.reshape(b * s, da), yb.reshape(b * s, dbp), yc.reshape(b * s, dbp),
                      g, wo_all, l, final_norm_w[None, :], final=(l == depth - 1))
    return x2.reshape(b, s, d)
```

```python
import functools
import itertools
from typing import NamedTuple

import jax
import jax.numpy as jnp
from jax import lax
from jax.experimental import pallas as pl
from jax.experimental.pallas import tpu as pltpu

HEAD_DIM = 64
LANES = 128
NORM_EPS = 1e-6
GN_EPS = 64e-5
VMEM_LIMIT = 56 * 1024 * 1024
EXP_ZERO = -104.0
SHIFT_SAFE = 80.0
NORM_SLACK = 1.01

F32 = jnp.float32
BF16 = jnp.bfloat16


class _Head(NamedTuple):
    q: tuple
    k: tuple
    v: tuple
    out: tuple
    c: int = -1


def _slot(col):
    assert col % HEAD_DIM == 0
    return (col // LANES, (col % LANES) // HEAD_DIM)


def _iota(shape, dim):
    return lax.broadcasted_iota(jnp.int32, shape, dim)


def _blk(idx, size):
    assert size & (size - 1) == 0
    return jnp.right_shift(idx, size.bit_length() - 1)


def _rem(idx, size):
    assert size & (size - 1) == 0
    return jnp.bitwise_and(idx, size - 1)


def _lanes(blk):
    return slice(blk * LANES, (blk + 1) * LANES)


def _dot(a, b):
    return jnp.dot(a.astype(BF16), b.astype(BF16), preferred_element_type=F32)


def _dot_nt(a, b):
    return lax.dot_general(a.astype(BF16), b.astype(BF16), (((1,), (1,)), ((), ())),
                           preferred_element_type=F32)


def _split(x, n):
    parts = []
    rem = x
    for i in range(n):
        p = rem.astype(BF16)
        parts.append(p)
        if i + 1 < n:
            rem = rem - p.astype(F32)
    return parts


def _dot_xr(x, r, n):
    out = None
    for p in _split(x, n):
        t = jnp.dot(p, r, preferred_element_type=F32)
        out = t if out is None else out + t
    return out


def _dot_lx(l, x, n):
    out = None
    for p in _split(x, n):
        t = jnp.dot(l, p, preferred_element_type=F32)
        out = t if out is None else out + t
    return out


def _log_sigmoid(z):
    return jnp.minimum(z, 0.0) - jnp.log1p(jnp.exp(-jnp.abs(z)))


def _sigmoid(z):
    return 1.0 / (1.0 + jnp.exp(-z))


def _head_lanes(lane, h):
    return (lane >= h * HEAD_DIM) & (lane < (h + 1) * HEAD_DIM)


def _query_in_key_half(q2, lane, hd):
    own = jnp.where(_head_lanes(lane, hd.q[1]), q2, jnp.zeros_like(q2))
    if hd.q[1] != hd.k[1]:
        own = pltpu.roll(own.astype(F32), HEAD_DIM, 1).astype(BF16)
    return own


def _write_heads(o_ref, rows, lane, heads, outs):
    t = lane.shape[0]
    for blk in sorted({hd.out[0] for hd in heads}):
        halves = {}
        for hd, o in zip(heads, outs):
            assert hd.out[1] == hd.v[1]
            if hd.out[0] == blk:
                halves[hd.out[1]] = o
        zero = jnp.zeros((t, LANES), F32)
        o_ref[0, rows, _lanes(blk)] = jnp.where(
            lane < HEAD_DIM, halves.get(0, zero), halves.get(1, zero)).astype(o_ref.dtype)


def _cparams(sem):
    return pltpu.CompilerParams(dimension_semantics=sem, vmem_limit_bytes=VMEM_LIMIT)


def _wprep_in_kernel(w_ref, o_ref, tail_ref, *, n_main, q_ranges, f_bases):
    hc = len(f_bases)
    n_in = w_ref.shape[-1]
    n_full = (n_in - n_main) // LANES * LANES
    n_tail = n_in - n_main - n_full
    n_src = n_full + LANES
    n_dst = o_ref.shape[-1] - n_main
    col = _iota((1, n_main), 1)
    is_q = None
    for a, b in q_ranges:
        m = (col >= a) & (col < b)
        is_q = m if is_q is None else (is_q | m)
    scale = jnp.where(is_q, HEAD_DIM ** -0.5, 1.0).astype(F32)
    o_ref[0, :, 0:n_main] = (w_ref[0, :, 0:n_main] * scale).astype(BF16)
    tail_ref[...] = jnp.zeros_like(tail_ref)
    tail_ref[:, 0:n_tail] = w_ref[0, :, n_main + n_full:n_in]
    x = jnp.concatenate([w_ref[0, :, n_main:n_main + n_full], tail_ref[...]], axis=1).astype(BF16)
    src = _iota((n_src, n_dst), 0)
    dst = _iota((n_src, n_dst), 1)
    gmap = src - hc + LANES
    fbase = jnp.zeros_like(src)
    for j, base in enumerate(f_bases):
        fbase = jnp.where(src == j, base, fbase)
    hit = ((src < hc) & (dst >= fbase) & (dst < fbase + 6)) | ((src >= hc) & (dst == gmap))
    move = (hit & (src < n_in - n_main)).astype(BF16)
    o_ref[0, :, n_main:] = jnp.dot(x, move, preferred_element_type=F32).astype(BF16)


def _wprep_in(w_in, n_main, q_ranges, f_bases, tr=256):
    depth, d, n_in = w_in.shape
    n_out = n_main + LANES + (n_in - n_main - len(f_bases))
    assert n_main % LANES == 0 and n_out % LANES == 0 and d % tr == 0
    return pl.pallas_call(
        functools.partial(_wprep_in_kernel, n_main=n_main, q_ranges=q_ranges, f_bases=f_bases),
        grid=(depth, d // tr),
        in_specs=[pl.BlockSpec((1, tr, n_in), lambda l, i: (l, i, 0))],
        out_specs=pl.BlockSpec((1, tr, n_out), lambda l, i: (l, i, 0)),
        out_shape=jax.ShapeDtypeStruct((depth, d, n_out), BF16),
        scratch_shapes=[pltpu.VMEM((tr, LANES), F32)],
        compiler_params=_cparams(("parallel", "parallel")),
        name="wprep_in",
    )(w_in)


def _wprep_out_kernel(w_ref, o_ref):
    o_ref[...] = w_ref[...].astype(BF16)


def _wprep_out(w_out):
    depth, k, d = w_out.shape
    return pl.pallas_call(
        _wprep_out_kernel,
        grid=(depth,),
        in_specs=[pl.BlockSpec((1, k, d), lambda l: (l, 0, 0))],
        out_specs=pl.BlockSpec((1, k, d), lambda l: (l, 0, 0)),
        out_shape=jax.ShapeDtypeStruct((depth, k, d), BF16),
        compiler_params=_cparams(("parallel",)),
        name="wprep_out",
    )(w_out)


def _inproj_kernel(x_ref, nw_ref, w_ref, *out_refs, n_chunk):
    x = x_ref[...]
    xn = x * lax.rsqrt(jnp.mean(x * x, axis=-1, keepdims=True) + NORM_EPS)
    h = (xn * nw_ref[...]).astype(BF16)
    outs, off = [], 0
    for ref in out_refs:
        outs.append((ref, off, off + ref.shape[-1]))
        off += ref.shape[-1]
    for c0 in range(0, off, n_chunk):
        c1 = min(c0 + n_chunk, off)
        res = jnp.dot(h, w_ref[:, c0:c1], preferred_element_type=F32)
        for ref, o0, o1 in outs:
            a, b = max(c0, o0), min(c1, o1)
            if a < b:
                ref[:, a - o0:b - o0] = res[:, a - c0:b - c0].astype(ref.dtype)


def _inproj(x2, nw, w, layer, widths, dtypes, tm=512, n_chunk=512):
    m, d = x2.shape
    n = w.shape[2]
    assert sum(widths) == n and m % tm == 0
    out_shape = [jax.ShapeDtypeStruct((m, wd), dt) for wd, dt in zip(widths, dtypes)]
    out_specs = [pl.BlockSpec((tm, wd), lambda i: (i, 0)) for wd in widths]
    return pl.pallas_call(
        functools.partial(_inproj_kernel, n_chunk=n_chunk),
        grid=(m // tm,),
        in_specs=[pl.BlockSpec((tm, d), lambda i: (i, 0)),
                  pl.BlockSpec((1, d), lambda i: (0, 0)),
                  pl.BlockSpec((None, d, n), lambda i: (layer, 0, 0),
                               pipeline_mode=pl.Buffered(1))],
        out_specs=out_specs,
        out_shape=out_shape,
        compiler_params=_cparams(("parallel",)),
        name="inproj",
    )(x2, nw, w)


def _outproj_kernel(x_ref, ya_ref, yb_ref, yc_ref, g_ref, w_ref, fw_ref, o_ref, *, final):
    g = g_ref[...].astype(F32)
    nb = yb_ref.shape[-1] - LANES
    y = jnp.concatenate([ya_ref[...], yb_ref[:, 0:nb], yb_ref[:, nb:] + yc_ref[:, 0:LANES],
                         yc_ref[:, LANES:]], axis=1).astype(F32)
    yg = (y * (g * _sigmoid(g))).astype(BF16)
    acc = x_ref[...] + jnp.dot(yg, w_ref[...], preferred_element_type=F32)
    if final:
        acc = acc * lax.rsqrt(jnp.mean(acc * acc, axis=-1, keepdims=True) + NORM_EPS) * fw_ref[...]
    o_ref[...] = acc


def _outproj(x2, ya, yb, yc, g, w, layer, fw, final, tm=1024):
    m, d = x2.shape
    row = lambda wd: pl.BlockSpec((tm, wd), lambda i: (i, 0))
    return pl.pallas_call(
        functools.partial(_outproj_kernel, final=final),
        grid=(m // tm,),
        in_specs=[row(d), row(ya.shape[1]), row(yb.shape[1]), row(yc.shape[1]), row(g.shape[1]),
                  pl.BlockSpec((None,) + w.shape[1:], lambda i: (layer, 0, 0)),
                  pl.BlockSpec((1, d), lambda i: (0, 0))],
        out_specs=row(d),
        out_shape=jax.ShapeDtypeStruct((m, d), F32),
        compiler_params=_cparams(("parallel",)),
        name="outproj",
    )(x2, ya, yb, yc, g, w, fw)


def _fcum_kernel(f_ref, bf_ref, c_ref, *, blk):
    s = f_ref.shape[1]
    row = _iota((blk, blk), 0)
    col = _iota((blk, blk), 1)
    tri = (col <= row).astype(BF16)
    carry = jnp.zeros((1, LANES), F32)
    for i in range(s // blk):
        lf = _log_sigmoid(f_ref[0, i * blk:(i + 1) * blk, :] + bf_ref[...])
        c = _dot_lx(tri, lf, 3) + carry
        c_ref[0, i * blk:(i + 1) * blk, :] = c
        carry = c[blk - 1:blk, :]


def _fcum(f, bf, blk=256):
    b, s, _ = f.shape
    return pl.pallas_call(
        functools.partial(_fcum_kernel, blk=blk),
        grid=(b,),
        in_specs=[pl.BlockSpec((1, s, LANES), lambda i: (i, 0, 0)),
                  pl.BlockSpec((1, LANES), lambda i: (0, 0))],
        out_specs=pl.BlockSpec((1, s, LANES), lambda i: (i, 0, 0)),
        out_shape=jax.ShapeDtypeStruct((b, s, LANES), F32),
        compiler_params=_cparams(("parallel",)),
        name="fcum",
    )(f, bf)


class _SbBlock:
    def __init__(self, q_ref, kv_ref, o_ref, qi, rows, heads):
        t = rows.stop - rows.start
        self.q_ref, self.kv_ref, self.o_ref = q_ref, kv_ref, o_ref
        self.qi, self.rows, self.heads, self.t = qi, rows, heads, t
        self.lane = _iota((t, LANES), 1)
        row = _iota((t, t), 0)
        col = _iota((t, t), 1)
        self.strict = col < row
        self.tri = (row > col).astype(BF16)

    def kv(self, k0, hd):
        rows = pl.ds(k0, self.t)
        return self.kv_ref[0, rows, _lanes(hd.k[0])], self.kv_ref[0, rows, _lanes(hd.v[0])]

    def logits(self, qh, k2, diag):
        z = _dot_nt(qh, k2)
        if diag:
            z = jnp.where(self.strict, z, -1e30)
        ls = jnp.minimum(z, 0.0) - jnp.log(1.0 + jnp.exp(-jnp.abs(z)))
        log1m = ls - z
        return ls, _dot_xr(log1m, self.tri, 1), jnp.sum(log1m, axis=-1, keepdims=True)

    def front(self):
        t, qi = self.t, self.qi
        q0 = pl.multiple_of(qi * t, t)
        p0 = pl.multiple_of(jnp.maximum(qi - 1, 0) * t, t)
        no_prev = jnp.where(qi > 0, 0.0, -1e30).astype(F32)
        self.qhs = [_query_in_key_half(self.q_ref[0, self.rows, _lanes(hd.q[0])], self.lane, hd)
                    for hd in self.heads]
        tiles = [(i, k0, diag) for i in range(len(self.heads)) for k0, diag in ((q0, True), (p0, False))]
        zs = []
        for i, k0, diag in tiles:
            z = _dot_nt(self.qhs[i], self.kv(k0, self.heads[i])[0])
            zs.append(jnp.where(self.strict, z, -1e30) if diag else z)
        yield
        lss = [jnp.minimum(z, 0.0) - jnp.log(1.0 + jnp.exp(-jnp.abs(z))) for z in zs]
        log1ms = [ls - z for ls, z in zip(lss, zs)]
        afters = [_dot_xr(l1, self.tri, 1) for l1 in log1ms]
        masses = [jnp.sum(l1, axis=-1, keepdims=True) for l1 in log1ms]
        yield
        attns = []
        for i in range(len(self.heads)):
            d, p = 2 * i, 2 * i + 1
            attns.append(jnp.exp(lss[d] + afters[d]))
            attns.append(jnp.exp(lss[p] + afters[p] + (masses[d] + no_prev)))
        yield
        self.accs = [_dot(attns[2 * i], self.kv(q0, hd)[1]) + _dot(attns[2 * i + 1], self.kv(p0, hd)[1])
                     for i, hd in enumerate(self.heads)]
        self.carries = [masses[2 * i] + masses[2 * i + 1] for i in range(len(self.heads))]
        yield

    def finish(self):
        t = self.t

        def any_live(carries):
            m = carries[0]
            for c in carries[1:]:
                m = jnp.maximum(m, c)
            return (jnp.max(m) > EXP_ZERO).astype(jnp.int32)

        def cond(st):
            return (st[0] >= 0) & (st[1] > 0)

        def body(st):
            kb, _, accs, carries = st
            k0 = pl.multiple_of(kb * t, t)
            new_accs, new_carries = [], []
            for hd, qh, acc, carry in zip(self.heads, self.qhs, accs, carries):
                k2, v2 = self.kv(k0, hd)
                ls, after, mass = self.logits(qh, k2, False)
                new_accs.append(acc + _dot(jnp.exp(ls + after + carry), v2))
                new_carries.append(carry + mass)
            return kb - 1, any_live(new_carries), tuple(new_accs), tuple(new_carries)

        st = lax.while_loop(cond, body, (self.qi - 2, any_live(self.carries), tuple(self.accs),
                                         tuple(self.carries)))
        _write_heads(self.o_ref, self.rows, self.lane, self.heads, st[2])


BIAS_GROUP = 8


def _bias_terms(val, lane, key_side):
    hi, mid, lo = _split(val, 3)
    pos = _rem(lane, BIAS_GROUP)
    one = jnp.ones((), BF16)
    if key_side:
        return jnp.where(pos < 3, one, jnp.where(pos == 3, -hi, jnp.where(pos == 4, -mid, -lo)))
    return jnp.where(pos == 0, hi, jnp.where(pos == 1, mid, jnp.where(pos == 2, lo, one)))


def _bias_group(lane, hd):
    return (lane >= hd.c) & (lane < hd.c + 6)


def _fox_kernel(q_ref, kv_ref, cq_ref, ck_ref, o_ref, kaug_ref, vaug_ref, acc_ref, kmax_ref,
                *, t, heads):
    qi = pl.program_id(1)
    s_len = kv_ref.shape[1]
    lane = _iota((t, LANES), 1)
    causal = _iota((t, t), 1) <= _iota((t, t), 0)
    q0 = pl.multiple_of(qi * t, t)
    ones_lane = [HEAD_DIM * (1 - hd.v[1]) for hd in heads]
    for hd in heads:
        assert hd.c % BIAS_GROUP == 0 and hd.c // HEAD_DIM == 1 - hd.k[1]

    half_of = _blk(_iota((LANES, LANES), 0), HEAD_DIM) == _blk(_iota((LANES, LANES), 1), HEAD_DIM)
    ones_half = half_of.astype(BF16)
    ones_all = jnp.ones((LANES, LANES), BF16)

    @pl.when(qi == 0)
    def _():
        lane_s = _iota((s_len, LANES), 1)
        lane_1 = _iota((1, LANES), 1)
        key_terms = _bias_terms(ck_ref[0], lane_s, key_side=True)
        norms = {}
        for i, hd in enumerate(heads):
            k2 = kv_ref[0, :, _lanes(hd.k[0])]
            if hd.k[0] not in norms:
                kf = k2.astype(F32)
                norms[hd.k[0]] = jnp.sqrt(jnp.max(_dot(kf * kf, ones_half), axis=0, keepdims=True))
            kmax_ref[i] = jnp.max(jnp.where(_head_lanes(lane_1, hd.k[1]),
                                            norms[hd.k[0]] * (NORM_SLACK * NORM_SLACK), 0.0))
            kaug_ref[i] = jnp.where(_bias_group(lane_s, hd), key_terms, k2)
            vaug_ref[i] = jnp.where(lane_s == ones_lane[i], jnp.ones((), BF16),
                                    kv_ref[0, :, _lanes(hd.v[0])])

    owns, ubound, shift = [], None, jnp.zeros((t, LANES), F32)
    for i, hd in enumerate(heads):
        own = _query_in_key_half(q_ref[0, :, _lanes(hd.q[0])], lane, hd)
        qf = own.astype(F32)
        u = jnp.sqrt(_dot(qf * qf, ones_all)) * kmax_ref[i]
        ubound = u if ubound is None else jnp.maximum(ubound, u)
        shift = jnp.where(_bias_group(lane, hd), u, shift)
        owns.append(own)
    query_terms = _bias_terms(cq_ref[0] - shift, lane, key_side=False)
    qas = [jnp.where(_bias_group(lane, hd), query_terms, own) for hd, own in zip(heads, owns)]
    safe = 2.0 * jnp.max(ubound) < SHIFT_SAFE

    def logits(i, k0, diag):
        s_blk = _dot_nt(qas[i], kaug_ref[i, pl.ds(k0, t), :])
        return jnp.where(causal, s_blk, -1e30) if diag else s_blk

    @pl.when(safe)
    def _():
        def sweep(k0, diag):
            ps = [jnp.exp(logits(i, k0, diag)) for i in range(len(heads))]
            for i, p in enumerate(ps):
                acc_ref[i] += _dot(p, vaug_ref[i, pl.ds(k0, t), :])

        acc_ref[...] = jnp.zeros_like(acc_ref)

        @pl.loop(0, qi)
        def _(j):
            sweep(pl.multiple_of(j * t, t), False)

        sweep(q0, True)
        outs = []
        for i in range(len(heads)):
            acc = acc_ref[i]
            l = jnp.sum(jnp.where(lane == ones_lane[i], acc, 0.0), axis=-1, keepdims=True)
            outs.append(acc * (1.0 / l))
        _write_heads(o_ref, slice(None), lane, heads, outs)

    @pl.when(jnp.logical_not(safe))
    def _():
        def update(st, s_blk, v2):
            m, l, acc = st
            m_new = jnp.maximum(m, jnp.max(s_blk, axis=-1, keepdims=True))
            alpha = jnp.exp(m - m_new)
            p = jnp.exp(s_blk - m_new)
            l = alpha * l + jnp.sum(p, axis=-1, keepdims=True)
            acc = alpha * acc + _dot(p, v2)
            return m_new, l, acc

        def sweep(k0, sts, diag):
            return tuple(update(st, logits(i, k0, diag), vaug_ref[i, pl.ds(k0, t), :])
                         for i, st in enumerate(sts))

        init = tuple((jnp.full((t, 1), -1e30, F32), jnp.zeros((t, 1), F32),
                      jnp.zeros((t, LANES), F32)) for _ in heads)
        sts = lax.fori_loop(0, qi, lambda j, sts: sweep(pl.multiple_of(j * t, t), sts, False),
                            init)
        sts = sweep(q0, sts, True)
        _write_heads(o_ref, slice(None), lane, heads, [acc * (1.0 / l) for _, l, acc in sts])


def _fox_attention(qkv, c, heads, n_out_blocks, t=512):
    b, s, w = qkv.shape
    return pl.pallas_call(
        functools.partial(_fox_kernel, t=t, heads=heads),
        grid=(b, s // t),
        in_specs=[pl.BlockSpec((1, t, w), lambda bi, qi: (bi, qi, 0)),
                  pl.BlockSpec((1, s, w), lambda bi, qi: (bi, 0, 0)),
                  pl.BlockSpec((1, t, LANES), lambda bi, qi: (bi, qi, 0)),
                  pl.BlockSpec((1, s, LANES), lambda bi, qi: (bi, 0, 0))],
        out_specs=pl.BlockSpec((1, t, n_out_blocks * LANES), lambda bi, qi: (bi, qi, 0)),
        out_shape=jax.ShapeDtypeStruct((b, s, n_out_blocks * LANES), BF16),
        scratch_shapes=[pltpu.VMEM((len(heads), s, LANES), BF16),
                        pltpu.VMEM((len(heads), s, LANES), BF16),
                        pltpu.VMEM((len(heads), t, LANES), F32),
                        pltpu.SMEM((len(heads),), F32)],
        compiler_params=_cparams(("parallel", "arbitrary")),
        name="fox",
    )(qkv, qkv, c, c)


def _unit_lower_inverses(lbds, c):
    n = lbds[0].shape[0]
    row = _iota((n, n), 0)
    col = _iota((n, n), 1)
    eye = (row == col).astype(F32)
    same2 = _blk(row, 2) == _blk(col, 2)
    xs = [eye + jnp.where(same2, l, 0.0) for l in lbds]
    s = 2
    while s < c:
        join = (_blk(row, 2 * s) == _blk(col, 2 * s)) & (_blk(row, s) != _blk(col, s))
        ts = [_dot(jnp.where(join, l, 0.0), x) for l, x in zip(lbds, xs)]
        yield True
        xs = [x + _dot(x, t) for x, t in zip(xs, ts)]
        yield True
        s *= 2
    return xs


def _rwkv_body(pa_ref, mu_ref, wl_ref, w0_ref, a0_ref, kk_ref, ka_ref, rk_ref, lnw_ref, lnb_ref,
               o_ref, prev_ref, h_ref, *, tb, c, da):
    n_pairs = da // LANES
    n_chunks = tb // c
    width = pa_ref.shape[-1]
    assert 2 * c == LANES

    p = pa_ref[0]
    prow = _iota((tb, width), 0)
    prev = jnp.where(prow == 0, prev_ref[...], pltpu.roll(p, 1, 0))
    prev_ref[...] = p[tb - 1:tb, :]
    xs = p + mu_ref[...] * (prev - p)
    r = xs[:, 0:da]
    k = xs[:, da:2 * da]
    v = xs[:, 2 * da:3 * da]
    wa = xs[:, 3 * da:3 * da + LANES]
    lane_t = _iota((tb, LANES), 1)
    wa = jnp.where(lane_t < HEAD_DIM, jnp.tanh(wa), wa)
    lora = _dot(wa, wl_ref[...])
    wpre = w0_ref[...] + lora[:, 0:da]
    w = -(jnp.maximum(-wpre, 0.0) + jnp.log(1.0 + jnp.exp(-jnp.abs(wpre)))) - 0.5
    lw = -jnp.exp(w)
    a = _sigmoid(a0_ref[...] + lora[:, da:2 * da])
    k2 = k * (1.0 + (a - 1.0) * ka_ref[...])
    kk = k * kk_ref[...]

    er = _blk(_iota((LANES, LANES), 0), HEAD_DIM)
    ec = _blk(_iota((LANES, LANES), 1), HEAD_DIM)
    bd_mask = er == ec
    ones_head = bd_mask.astype(BF16)
    diag_eye = (_iota((LANES, LANES), 0) == _iota((LANES, LANES), 1)).astype(F32)

    pairs = [_lanes(i) for i in range(n_pairs)]
    kkn = jnp.concatenate(
        [kk[:, sl] * lax.rsqrt(jnp.maximum(_dot_xr(kk[:, sl] * kk[:, sl], ones_head, 1), 1e-24))
         for sl in pairs], axis=1)
    av = -kkn
    bv = kkn * a

    grp = min(tb, 2 * LANES)
    tr = _iota((grp, grp), 0)
    tc = _iota((grp, grp), 1)
    tri_blk = ((_blk(tr, c) == _blk(tc, c)) & (tc <= tr)).astype(BF16)
    cum = jnp.concatenate([_dot_lx(tri_blk, lw[i:i + grp], 2) for i in range(0, tb, grp)], axis=0)
    eg = jnp.exp(cum)
    rt = r * eg
    at = av * jnp.exp(cum - lw)
    ieg = jnp.exp(-cum)
    kt = k2 * ieg
    bt = bv * ieg

    rr = _iota((2 * c, 2 * c), 0)
    cc = _iota((2 * c, 2 * c), 1)
    same_blk = _blk(rr, c) == _blk(cc, c)
    strict_2c = _rem(cc, c) < _rem(rr, c)
    incl_2c = _rem(cc, c) <= _rem(rr, c)
    m0 = _iota((c, LANES), 1) < HEAD_DIM
    m0w = _rem(_iota((c, 2 * LANES), 1), LANES) < HEAD_DIM
    zeros_c = jnp.zeros((c, LANES), F32)

    items = [(sl, slice(j * c, (j + 1) * c)) for sl in pairs for j in range(n_chunks)]
    lbds, laks, prms, qa2s, vstk, gams, bkbars = [], [], [], [], [], [], []
    for sl, rs in items:
        rc, ac, kc, bc, vc = rt[rs, sl], at[rs, sl], kt[rs, sl], bt[rs, sl], v[rs, sl]
        gam = eg[rs.stop - 1:rs.stop, sl]
        bk = jnp.concatenate([bc, kc], axis=0)
        at0 = jnp.where(m0, ac, 0.0)
        at1 = jnp.where(m0, 0.0, ac)
        qa4 = jnp.concatenate([at0, at1, jnp.where(m0, rc, 0.0), jnp.where(m0, 0.0, rc)], axis=0)
        pm = _dot_nt(qa4, bk)
        xall = jnp.concatenate([pm[0:c], pltpu.roll(pm[c:2 * c], c, 1)], axis=0)
        lbds.append(jnp.where(same_blk & strict_2c, xall, 0.0))
        laks.append(jnp.where((~same_blk) & strict_2c, xall, 0.0))
        prms.append(jnp.where(incl_2c, pm[2 * c:], 0.0))
        qa2s.append(jnp.concatenate([at0, at1], axis=0))
        vstk.append(jnp.concatenate([jnp.where(m0, 0.0, vc), jnp.where(m0, vc, 0.0)], axis=0))
        gams.append(gam)
        bkbars.append(jnp.concatenate([bc * gam, kc * gam], axis=0))

    yield True
    tinvs = yield from _unit_lower_inverses(lbds, c)
    w1s = [_dot(lak, vs) for lak, vs in zip(laks, vstk)]
    tzs = [_dot(tinv, jnp.concatenate([qa2, w1], axis=1))
           for tinv, qa2, w1 in zip(tinvs, qa2s, w1s)]
    r3s, y2s, phis, gs = [], [], [], []
    for (sl, rs), tz, prm, bkbar, gam in zip(items, tzs, prms, bkbars, gams):
        a2v2 = tz[0:c] + tz[c:2 * c]
        rhs = jnp.concatenate([a2v2, jnp.concatenate([zeros_c, v[rs, sl]], axis=1)], axis=0)
        res = _dot(jnp.concatenate([prm, bkbar.T], axis=0), rhs)
        ry = jnp.where(m0w, res[0:c], res[c:2 * c])
        r3s.append(rt[rs, sl] + ry[:, 0:LANES])
        y2s.append(ry[:, LANES:])
        pg = res[2 * c:]
        phis.append(diag_eye * gam + jnp.where(bd_mask, pg[:, 0:LANES], 0.0))
        gs.append(jnp.where(bd_mask, pg[:, LANES:], 0.0))

    hs = [h_ref[i] for i in range(n_pairs)]
    ys = [[] for _ in range(n_pairs)]
    for j in range(n_chunks):
        for i in range(n_pairs):
            idx = i * n_chunks + j
            ys[i].append(_dot(r3s[idx], hs[i]) + y2s[idx])
        hs = [_dot(phis[i * n_chunks + j], hs[i]) + gs[i * n_chunks + j] for i in range(n_pairs)]
    for i, sl in enumerate(pairs):
        h_ref[i] = hs[i]
        y = jnp.concatenate(ys[i], axis=0)
        mean = _dot_xr(y, ones_head, 2) * (1.0 / HEAD_DIM)
        yc_ = y - mean
        var = _dot_xr(yc_ * yc_, ones_head, 1) * (1.0 / HEAD_DIM)
        yn = yc_ * lax.rsqrt(var + GN_EPS) * lnw_ref[:, sl] + lnb_ref[:, sl]
        bonus = _dot_xr(r[:, sl] * k2[:, sl] * rk_ref[:, sl], ones_head, 1) * v[:, sl]
        o_ref[0, :, sl] = (yn + bonus).astype(o_ref.dtype)


def _rwkv_sb_kernel(*refs, tb, c, da, t, heads):
    rwkv_in, (q_ref, kv_ref, ya_ref, yb_ref, prev_ref, h_ref) = refs[:10], refs[10:]
    ti = pl.program_id(1)

    @pl.when(ti == 0)
    def _():
        prev_ref[...] = jnp.zeros_like(prev_ref)
        h_ref[...] = jnp.zeros_like(h_ref)

    blocks = [_SbBlock(q_ref, kv_ref, yb_ref, ti * (tb // t) + j, slice(j * t, (j + 1) * t), heads)
              for j in range(tb // t)]
    sb_heads = itertools.chain(*[blk.front() for blk in blocks])
    for mxu_run_next in _rwkv_body(*rwkv_in, ya_ref, prev_ref, h_ref, tb=tb, c=c, da=da):
        if mxu_run_next:
            next(sb_heads, None)
    for _ in sb_heads:
        pass
    for blk in blocks:
        blk.finish()


def _rwkv_sb(pa, mu, wl, w0, a0, k_k, k_a, r_k, ln_w, ln_b, qkv, heads, n_out_blocks, sb_width,
             tb=512, c=64, t=256):
    b, s, width = pa.shape
    da = w0.shape[-1]
    vec = lambda n: pl.BlockSpec((1, n), lambda bi, ti: (0, 0))
    return pl.pallas_call(
        functools.partial(_rwkv_sb_kernel, tb=tb, c=c, da=da, t=t, heads=heads),
        grid=(b, s // tb),
        in_specs=[pl.BlockSpec((1, tb, width), lambda bi, ti: (bi, ti, 0)),
                  vec(width),
                  pl.BlockSpec(wl.shape, lambda bi, ti: (0, 0)),
                  vec(da), vec(da), vec(da), vec(da), vec(da), vec(da), vec(da),
                  pl.BlockSpec((1, tb, sb_width), lambda bi, ti: (bi, ti, 0)),
                  pl.BlockSpec((1, s, sb_width), lambda bi, ti: (bi, 0, 0))],
        out_specs=[pl.BlockSpec((1, tb, da), lambda bi, ti: (bi, ti, 0)),
                   pl.BlockSpec((1, tb, n_out_blocks * LANES), lambda bi, ti: (bi, ti, 0))],
        out_shape=[jax.ShapeDtypeStruct((b, s, da), BF16),
                   jax.ShapeDtypeStruct((b, s, n_out_blocks * LANES), BF16)],
        scratch_shapes=[pltpu.VMEM((1, width), F32),
                        pltpu.VMEM((da // LANES, LANES, LANES), F32)],
        compiler_params=_cparams(("parallel", "arbitrary")),
        name="rwkv7_stickbreak",
    )(pa, mu, wl, w0, a0, k_k, k_a, r_k, ln_w, ln_b, qkv, qkv)


def kernel(x, norm_w, w_in, b_f, mu, w0, w_up, a0, a_up, k_k, k_a, r_k, ln_x_w, ln_x_b, w_out,
           final_norm_w):
    b, s, d = x.shape
    depth = w_in.shape[0]
    da = w0.shape[-1]
    lora = w_up.shape[1]
    hc = b_f.shape[-1]
    d_mix = w_out.shape[1]
    db = (d_mix - da) // 2
    n_shift = 3 * da + 2 * lora
    n_qkv = 6 * db
    assert da % LANES == 0 and 2 * lora == LANES and db == hc * HEAD_DIM and hc % 2 == 1
    assert mu.shape[-1] == n_shift and (n_shift + n_qkv) % LANES == 0
    n_out_blocks = (db + HEAD_DIM) // LANES
    dbp = n_out_blocks * LANES

    n_main = n_shift + n_qkv
    q_ranges = ((n_shift, n_shift + db), (n_shift + 3 * db, n_shift + 4 * db))
    sb_heads = tuple(_Head(q=_slot(HEAD_DIM * g), k=_slot(db + HEAD_DIM * g),
                           v=_slot(2 * db + HEAD_DIM * g), out=_slot(HEAD_DIM * g))
                     for g in range(hc))
    fox_heads = tuple(
        _Head(q=_slot(3 * db + HEAD_DIM * j), k=_slot(4 * db + HEAD_DIM * j),
              v=_slot(5 * db + HEAD_DIM * j), out=_slot(HEAD_DIM * (j + 1)),
              c=HEAD_DIM * (1 - _slot(4 * db + HEAD_DIM * j)[1]) + BIAS_GROUP * j)
        for j in range(hc))
    f_bases = tuple(hd.c for hd in fox_heads)
    assert hc * BIAS_GROUP <= HEAD_DIM
    w_all = _wprep_in(w_in, n_main, q_ranges, f_bases)
    wo_all = _wprep_out(w_out)
    sb_width = -(-(3 * db) // (2 * LANES)) * 2 * LANES
    assert all(hd.out[1] == hd.v[1] for hd in sb_heads + fox_heads) and sb_width <= n_qkv

    x2 = x.reshape(b * s, d)
    zpad = lambda rows, n: jnp.zeros((rows, n), F32)
    for l in range(depth):
        widths = (n_shift, n_qkv, LANES, d_mix)
        dtypes = (F32, BF16, F32, BF16)
        pa, qkv, f, g = _inproj(x2, norm_w[l][None, :], w_all, l, widths, dtypes)

        wl_comb = jnp.concatenate(
            [jnp.concatenate([w_up[l], zpad(lora, da)], axis=1),
             jnp.concatenate([zpad(lora, da), a_up[l]], axis=1)], axis=0).astype(BF16)
        row = lambda t: t.reshape(1, -1)
        qkv3 = qkv.reshape(b, s, n_qkv)
        ya, yb = _rwkv_sb(pa.reshape(b, s, n_shift), row(mu[l]), wl_comb, row(w0[l]), row(a0[l]),
                          row(k_k[l]), row(k_a[l]), row(r_k[l]), row(ln_x_w[l]), row(ln_x_b[l]),
                          qkv3, sb_heads, n_out_blocks, sb_width)
        lane_ids = jnp.arange(LANES)
        bf_pad = jnp.zeros((LANES,), F32)
        for j, base in enumerate(f_bases):
            bf_pad = jnp.where((lane_ids >= base) & (lane_ids < base + 6), b_f[l, j], bf_pad)
        cum_f = _fcum(f.reshape(b, s, LANES), bf_pad[None, :])
        yc = _fox_attention(qkv3, cum_f, fox_heads, n_out_blocks)

        x2 = _outproj(x2, ya.reshape(b * s, da), yb.reshape(b * s, dbp), yc.reshape(b * s, dbp),
                      g, wo_all, l, final_norm_w[None, :], final=(l == depth - 1))
    return x2.reshape(b, s, d)
```

```python
import functools
import itertools
from typing import NamedTuple

import jax
import jax.numpy as jnp
from jax import lax
from jax.experimental import pallas as pl
from jax.experimental.pallas import tpu as pltpu

HEAD_DIM = 64
LANES = 128
NORM_EPS = 1e-6
GN_EPS = 64e-5
VMEM_LIMIT = 56 * 1024 * 1024
EXP_ZERO = -104.0
SHIFT_SAFE = 80.0
NORM_SLACK = 1.01

F32 = jnp.float32
BF16 = jnp.bfloat16


class _Head(NamedTuple):
    q: tuple
    k: tuple
    v: tuple
    out: tuple
    c: int = -1


def _slot(col):
    assert col % HEAD_DIM == 0
    return (col // LANES, (col % LANES) // HEAD_DIM)


def _iota(shape, dim):
    return lax.broadcasted_iota(jnp.int32, shape, dim)


def _blk(idx, size):
    assert size & (size - 1) == 0
    return jnp.right_shift(idx, size.bit_length() - 1)


def _rem(idx, size):
    assert size & (size - 1) == 0
    return jnp.bitwise_and(idx, size - 1)


def _lanes(blk):
    return slice(blk * LANES, (blk + 1) * LANES)


def _dot(a, b):
    return jnp.dot(a.astype(BF16), b.astype(BF16), preferred_element_type=F32)


def _dot_nt(a, b):
    return lax.dot_general(a.astype(BF16), b.astype(BF16), (((1,), (1,)), ((), ())),
                           preferred_element_type=F32)


def _split(x, n):
    parts = []
    rem = x
    for i in range(n):
        p = rem.astype(BF16)
        parts.append(p)
        if i + 1 < n:
            rem = rem - p.astype(F32)
    return parts


def _dot_xr(x, r, n):
    out = None
    for p in _split(x, n):
        t = jnp.dot(p, r, preferred_element_type=F32)
        out = t if out is None else out + t
    return out


def _dot_lx(l, x, n):
    out = None
    for p in _split(x, n):
        t = jnp.dot(l, p, preferred_element_type=F32)
        out = t if out is None else out + t
    return out


def _log_sigmoid(z):
    return jnp.minimum(z, 0.0) - jnp.log1p(jnp.exp(-jnp.abs(z)))


def _sigmoid(z):
    return 1.0 / (1.0 + jnp.exp(-z))


def _head_lanes(lane, h):
    return (lane >= h * HEAD_DIM) & (lane < (h + 1) * HEAD_DIM)


def _query_in_key_half(q2, lane, hd):
    own = jnp.where(_head_lanes(lane, hd.q[1]), q2, jnp.zeros_like(q2))
    if hd.q[1] != hd.k[1]:
        own = pltpu.roll(own.astype(F32), HEAD_DIM, 1).astype(BF16)
    return own


def _write_heads(o_ref, rows, lane, heads, outs):
    t = lane.shape[0]
    for blk in sorted({hd.out[0] for hd in heads}):
        halves = {}
        for hd, o in zip(heads, outs):
            assert hd.out[1] == hd.v[1]
            if hd.out[0] == blk:
                halves[hd.out[1]] = o
        zero = jnp.zeros((t, LANES), F32)
        o_ref[0, rows, _lanes(blk)] = jnp.where(
            lane < HEAD_DIM, halves.get(0, zero), halves.get(1, zero)).astype(o_ref.dtype)


def _cparams(sem):
    return pltpu.CompilerParams(dimension_semantics=sem, vmem_limit_bytes=VMEM_LIMIT)


def _wprep_in_kernel(w_ref, o_ref, tail_ref, *, n_main, q_ranges, f_bases):
    hc = len(f_bases)
    n_in = w_ref.shape[-1]
    n_full = (n_in - n_main) // LANES * LANES
    n_tail = n_in - n_main - n_full
    n_src = n_full + LANES
    n_dst = o_ref.shape[-1] - n_main
    col = _iota((1, n_main), 1)
    is_q = None
    for a, b in q_ranges:
        m = (col >= a) & (col < b)
        is_q = m if is_q is None else (is_q | m)
    scale = jnp.where(is_q, HEAD_DIM ** -0.5, 1.0).astype(F32)
    o_ref[0, :, 0:n_main] = (w_ref[0, :, 0:n_main] * scale).astype(BF16)
    tail_ref[...] = jnp.zeros_like(tail_ref)
    tail_ref[:, 0:n_tail] = w_ref[0, :, n_main + n_full:n_in]
    x = jnp.concatenate([w_ref[0, :, n_main:n_main + n_full], tail_ref[...]], axis=1).astype(BF16)
    src = _iota((n_src, n_dst), 0)
    dst = _iota((n_src, n_dst), 1)
    gmap = src - hc + LANES
    fbase = jnp.zeros_like(src)
    for j, base in enumerate(f_bases):
        fbase = jnp.where(src == j, base, fbase)
    hit = ((src < hc) & (dst >= fbase) & (dst < fbase + 6)) | ((src >= hc) & (dst == gmap))
    move = (hit & (src < n_in - n_main)).astype(BF16)
    o_ref[0, :, n_main:] = jnp.dot(x, move, preferred_element_type=F32).astype(BF16)


def _wprep_in(w_in, n_main, q_ranges, f_bases, tr=256):
    depth, d, n_in = w_in.shape
    n_out = n_main + LANES + (n_in - n_main - len(f_bases))
    assert n_main % LANES == 0 and n_out % LANES == 0 and d % tr == 0
    return pl.pallas_call(
        functools.partial(_wprep_in_kernel, n_main=n_main, q_ranges=q_ranges, f_bases=f_bases),
        grid=(depth, d // tr),
        in_specs=[pl.BlockSpec((1, tr, n_in), lambda l, i: (l, i, 0))],
        out_specs=pl.BlockSpec((1, tr, n_out), lambda l, i: (l, i, 0)),
        out_shape=jax.ShapeDtypeStruct((depth, d, n_out), BF16),
        scratch_shapes=[pltpu.VMEM((tr, LANES), F32)],
        compiler_params=_cparams(("parallel", "parallel")),
        name="wprep_in",
    )(w_in)


def _wprep_out_kernel(w_ref, o_ref):
    o_ref[...] = w_ref[...].astype(BF16)


def _wprep_out(w_out):
    depth, k, d = w_out.shape
    return pl.pallas_call(
        _wprep_out_kernel,
        grid=(depth,),
        in_specs=[pl.BlockSpec((1, k, d), lambda l: (l, 0, 0))],
        out_specs=pl.BlockSpec((1, k, d), lambda l: (l, 0, 0)),
        out_shape=jax.ShapeDtypeStruct((depth, k, d), BF16),
        compiler_params=_cparams(("parallel",)),
        name="wprep_out",
    )(w_out)


def _inproj_kernel(x_ref, nw_ref, w_ref, *out_refs, n_chunk):
    x = x_ref[...]
    xn = x * lax.rsqrt(jnp.mean(x * x, axis=-1, keepdims=True) + NORM_EPS)
    h = (xn * nw_ref[...]).astype(BF16)
    outs, off = [], 0
    for ref in out_refs:
        outs.append((ref, off, off + ref.shape[-1]))
        off += ref.shape[-1]
    for c0 in range(0, off, n_chunk):
        c1 = min(c0 + n_chunk, off)
        res = jnp.dot(h, w_ref[:, c0:c1], preferred_element_type=F32)
        for ref, o0, o1 in outs:
            a, b = max(c0, o0), min(c1, o1)
            if a < b:
                ref[:, a - o0:b - o0] = res[:, a - c0:b - c0].astype(ref.dtype)


def _inproj(x2, nw, w, layer, widths, dtypes, tm=1024, n_chunk=512):
    m, d = x2.shape
    n = w.shape[2]
    assert sum(widths) == n and m % tm == 0
    out_shape = [jax.ShapeDtypeStruct((m, wd), dt) for wd, dt in zip(widths, dtypes)]
    out_specs = [pl.BlockSpec((tm, wd), lambda i: (i, 0)) for wd in widths]
    return pl.pallas_call(
        functools.partial(_inproj_kernel, n_chunk=n_chunk),
        grid=(m // tm,),
        in_specs=[pl.BlockSpec((tm, d), lambda i: (i, 0)),
                  pl.BlockSpec((1, d), lambda i: (0, 0)),
                  pl.BlockSpec((None, d, n), lambda i: (layer, 0, 0),
                               pipeline_mode=pl.Buffered(1))],
        out_specs=out_specs,
        out_shape=out_shape,
        compiler_params=_cparams(("parallel",)),
        name="inproj",
    )(x2, nw, w)


def _outproj_kernel(x_ref, ya_ref, yb_ref, yc_ref, g_ref, w_ref, fw_ref, o_ref, *, final):
    g = g_ref[...].astype(F32)
    nb = yb_ref.shape[-1] - LANES
    y = jnp.concatenate([ya_ref[...], yb_ref[:, 0:nb], yb_ref[:, nb:] + yc_ref[:, 0:LANES],
                         yc_ref[:, LANES:]], axis=1).astype(F32)
    yg = (y * (g * _sigmoid(g))).astype(BF16)
    acc = x_ref[...] + jnp.dot(yg, w_ref[...], preferred_element_type=F32)
    if final:
        acc = acc * lax.rsqrt(jnp.mean(acc * acc, axis=-1, keepdims=True) + NORM_EPS) * fw_ref[...]
    o_ref[...] = acc


def _outproj(x2, ya, yb, yc, g, w, layer, fw, final, tm=1024):
    m, d = x2.shape
    row = lambda wd: pl.BlockSpec((tm, wd), lambda i: (i, 0))
    return pl.pallas_call(
        functools.partial(_outproj_kernel, final=final),
        grid=(m // tm,),
        in_specs=[row(d), row(ya.shape[1]), row(yb.shape[1]), row(yc.shape[1]), row(g.shape[1]),
                  pl.BlockSpec((None,) + w.shape[1:], lambda i: (layer, 0, 0)),
                  pl.BlockSpec((1, d), lambda i: (0, 0))],
        out_specs=row(d),
        out_shape=jax.ShapeDtypeStruct((m, d), F32),
        compiler_params=_cparams(("parallel",)),
        name="outproj",
    )(x2, ya, yb, yc, g, w, fw)


def _fcum_kernel(f_ref, bf_ref, c_ref, *, blk):
    s = f_ref.shape[1]
    row = _iota((blk, blk), 0)
    col = _iota((blk, blk), 1)
    tri = (col <= row).astype(BF16)
    carry = jnp.zeros((1, LANES), F32)
    for i in range(s // blk):
        lf = _log_sigmoid(f_ref[0, i * blk:(i + 1) * blk, :] + bf_ref[...])
        c = _dot_lx(tri, lf, 3) + carry
        c_ref[0, i * blk:(i + 1) * blk, :] = c
        carry = c[blk - 1:blk, :]


def _fcum(f, bf, blk=256):
    b, s, _ = f.shape
    return pl.pallas_call(
        functools.partial(_fcum_kernel, blk=blk),
        grid=(b,),
        in_specs=[pl.BlockSpec((1, s, LANES), lambda i: (i, 0, 0)),
                  pl.BlockSpec((1, LANES), lambda i: (0, 0))],
        out_specs=pl.BlockSpec((1, s, LANES), lambda i: (i, 0, 0)),
        out_shape=jax.ShapeDtypeStruct((b, s, LANES), F32),
        compiler_params=_cparams(("parallel",)),
        name="fcum",
    )(f, bf)


class _SbBlock:
    def __init__(self, q_ref, kv_ref, o_ref, qi, rows, heads):
        t = rows.stop - rows.start
        self.q_ref, self.kv_ref, self.o_ref = q_ref, kv_ref, o_ref
        self.qi, self.rows, self.heads, self.t = qi, rows, heads, t
        self.lane = _iota((t, LANES), 1)
        row = _iota((t, t), 0)
        col = _iota((t, t), 1)
        self.strict = col < row
        self.tri = (row > col).astype(BF16)

    def kv(self, k0, hd):
        rows = pl.ds(k0, self.t)
        return self.kv_ref[0, rows, _lanes(hd.k[0])], self.kv_ref[0, rows, _lanes(hd.v[0])]

    def logits(self, qh, k2, diag):
        z = _dot_nt(qh, k2)
        if diag:
            z = jnp.where(self.strict, z, -1e30)
        ls = jnp.minimum(z, 0.0) - jnp.log(1.0 + jnp.exp(-jnp.abs(z)))
        log1m = ls - z
        return ls, _dot_xr(log1m, self.tri, 1), jnp.sum(log1m, axis=-1, keepdims=True)

    def front(self):
        t, qi = self.t, self.qi
        q0 = pl.multiple_of(qi * t, t)
        p0 = pl.multiple_of(jnp.maximum(qi - 1, 0) * t, t)
        no_prev = jnp.where(qi > 0, 0.0, -1e30).astype(F32)
        self.qhs = [_query_in_key_half(self.q_ref[0, self.rows, _lanes(hd.q[0])], self.lane, hd)
                    for hd in self.heads]
        tiles = [(i, k0, diag) for i in range(len(self.heads)) for k0, diag in ((q0, True), (p0, False))]
        zs = []
        for i, k0, diag in tiles:
            z = _dot_nt(self.qhs[i], self.kv(k0, self.heads[i])[0])
            zs.append(jnp.where(self.strict, z, -1e30) if diag else z)
        yield
        lss = [jnp.minimum(z, 0.0) - jnp.log(1.0 + jnp.exp(-jnp.abs(z))) for z in zs]
        log1ms = [ls - z for ls, z in zip(lss, zs)]
        afters = [_dot_xr(l1, self.tri, 1) for l1 in log1ms]
        masses = [jnp.sum(l1, axis=-1, keepdims=True) for l1 in log1ms]
        yield
        attns = []
        for i in range(len(self.heads)):
            d, p = 2 * i, 2 * i + 1
            attns.append(jnp.exp(lss[d] + afters[d]))
            attns.append(jnp.exp(lss[p] + afters[p] + (masses[d] + no_prev)))
        yield
        self.accs = [_dot(attns[2 * i], self.kv(q0, hd)[1]) + _dot(attns[2 * i + 1], self.kv(p0, hd)[1])
                     for i, hd in enumerate(self.heads)]
        self.carries = [masses[2 * i] + masses[2 * i + 1] for i in range(len(self.heads))]
        yield

    def finish(self):
        t = self.t

        def any_live(carries):
            m = carries[0]
            for c in carries[1:]:
                m = jnp.maximum(m, c)
            return (jnp.max(m) > EXP_ZERO).astype(jnp.int32)

        def cond(st):
            return (st[0] >= 0) & (st[1] > 0)

        def body(st):
            kb, _, accs, carries = st
            k0 = pl.multiple_of(kb * t, t)
            new_accs, new_carries = [], []
            for hd, qh, acc, carry in zip(self.heads, self.qhs, accs, carries):
                k2, v2 = self.kv(k0, hd)
                ls, after, mass = self.logits(qh, k2, False)
                new_accs.append(acc + _dot(jnp.exp(ls + after + carry), v2))
                new_carries.append(carry + mass)
            return kb - 1, any_live(new_carries), tuple(new_accs), tuple(new_carries)

        st = lax.while_loop(cond, body, (self.qi - 2, any_live(self.carries), tuple(self.accs),
                                         tuple(self.carries)))
        _write_heads(self.o_ref, self.rows, self.lane, self.heads, st[2])


BIAS_GROUP = 8


def _bias_terms(val, lane, key_side):
    hi, mid, lo = _split(val, 3)
    pos = _rem(lane, BIAS_GROUP)
    one = jnp.ones((), BF16)
    if key_side:
        return jnp.where(pos < 3, one, jnp.where(pos == 3, -hi, jnp.where(pos == 4, -mid, -lo)))
    return jnp.where(pos == 0, hi, jnp.where(pos == 1, mid, jnp.where(pos == 2, lo, one)))


def _bias_group(lane, hd):
    return (lane >= hd.c) & (lane < hd.c + 6)


def _fox_kernel(q_ref, kv_ref, cq_ref, ck_ref, o_ref, kaug_ref, vaug_ref, acc_ref, kmax_ref,
                *, t, heads):
    qi = pl.program_id(1)
    s_len = kv_ref.shape[1]
    lane = _iota((t, LANES), 1)
    causal = _iota((t, t), 1) <= _iota((t, t), 0)
    q0 = pl.multiple_of(qi * t, t)
    ones_lane = [HEAD_DIM * (1 - hd.v[1]) for hd in heads]
    for hd in heads:
        assert hd.c % BIAS_GROUP == 0 and hd.c // HEAD_DIM == 1 - hd.k[1]

    half_of = _blk(_iota((LANES, LANES), 0), HEAD_DIM) == _blk(_iota((LANES, LANES), 1), HEAD_DIM)
    ones_half = half_of.astype(BF16)
    ones_all = jnp.ones((LANES, LANES), BF16)

    @pl.when(qi == 0)
    def _():
        lane_s = _iota((s_len, LANES), 1)
        lane_1 = _iota((1, LANES), 1)
        key_terms = _bias_terms(ck_ref[0], lane_s, key_side=True)
        norms = {}
        for i, hd in enumerate(heads):
            k2 = kv_ref[0, :, _lanes(hd.k[0])]
            if hd.k[0] not in norms:
                kf = k2.astype(F32)
                norms[hd.k[0]] = jnp.sqrt(jnp.max(_dot(kf * kf, ones_half), axis=0, keepdims=True))
            kmax_ref[i] = jnp.max(jnp.where(_head_lanes(lane_1, hd.k[1]),
                                            norms[hd.k[0]] * (NORM_SLACK * NORM_SLACK), 0.0))
            kaug_ref[i] = jnp.where(_bias_group(lane_s, hd), key_terms, k2)
            vaug_ref[i] = jnp.where(lane_s == ones_lane[i], jnp.ones((), BF16),
                                    kv_ref[0, :, _lanes(hd.v[0])])

    owns, ubound, shift = [], None, jnp.zeros((t, LANES), F32)
    for i, hd in enumerate(heads):
        own = _query_in_key_half(q_ref[0, :, _lanes(hd.q[0])], lane, hd)
        qf = own.astype(F32)
        u = jnp.sqrt(_dot(qf * qf, ones_all)) * kmax_ref[i]
        ubound = u if ubound is None else jnp.maximum(ubound, u)
        shift = jnp.where(_bias_group(lane, hd), u, shift)
        owns.append(own)
    query_terms = _bias_terms(cq_ref[0] - shift, lane, key_side=False)
    qas = [jnp.where(_bias_group(lane, hd), query_terms, own) for hd, own in zip(heads, owns)]
    safe = 2.0 * jnp.max(ubound) < SHIFT_SAFE

    def logits(i, k0, diag):
        s_blk = _dot_nt(qas[i], kaug_ref[i, pl.ds(k0, t), :])
        return jnp.where(causal, s_blk, -1e30) if diag else s_blk

    @pl.when(safe)
    def _():
        def sweep(k0, diag):
            ps = [jnp.exp(logits(i, k0, diag)) for i in range(len(heads))]
            for i, p in enumerate(ps):
                acc_ref[i] += _dot(p, vaug_ref[i, pl.ds(k0, t), :])

        acc_ref[...] = jnp.zeros_like(acc_ref)

        @pl.loop(0, qi)
        def _(j):
            sweep(pl.multiple_of(j * t, t), False)

        sweep(q0, True)
        outs = []
        for i in range(len(heads)):
            acc = acc_ref[i]
            l = jnp.sum(jnp.where(lane == ones_lane[i], acc, 0.0), axis=-1, keepdims=True)
            outs.append(acc * (1.0 / l))
        _write_heads(o_ref, slice(None), lane, heads, outs)

    @pl.when(jnp.logical_not(safe))
    def _():
        def update(st, s_blk, v2):
            m, l, acc = st
            m_new = jnp.maximum(m, jnp.max(s_blk, axis=-1, keepdims=True))
            alpha = jnp.exp(m - m_new)
            p = jnp.exp(s_blk - m_new)
            l = alpha * l + jnp.sum(p, axis=-1, keepdims=True)
            acc = alpha * acc + _dot(p, v2)
            return m_new, l, acc

        def sweep(k0, sts, diag):
            return tuple(update(st, logits(i, k0, diag), vaug_ref[i, pl.ds(k0, t), :])
                         for i, st in enumerate(sts))

        init = tuple((jnp.full((t, 1), -1e30, F32), jnp.zeros((t, 1), F32),
                      jnp.zeros((t, LANES), F32)) for _ in heads)
        sts = lax.fori_loop(0, qi, lambda j, sts: sweep(pl.multiple_of(j * t, t), sts, False),
                            init)
        sts = sweep(q0, sts, True)
        _write_heads(o_ref, slice(None), lane, heads, [acc * (1.0 / l) for _, l, acc in sts])


def _fox_attention(qkv, c, heads, n_out_blocks, t=512):
    b, s, w = qkv.shape
    return pl.pallas_call(
        functools.partial(_fox_kernel, t=t, heads=heads),
        grid=(b, s // t),
        in_specs=[pl.BlockSpec((1, t, w), lambda bi, qi: (bi, qi, 0)),
                  pl.BlockSpec((1, s, w), lambda bi, qi: (bi, 0, 0)),
                  pl.BlockSpec((1, t, LANES), lambda bi, qi: (bi, qi, 0)),
                  pl.BlockSpec((1, s, LANES), lambda bi, qi: (bi, 0, 0))],
        out_specs=pl.BlockSpec((1, t, n_out_blocks * LANES), lambda bi, qi: (bi, qi, 0)),
        out_shape=jax.ShapeDtypeStruct((b, s, n_out_blocks * LANES), BF16),
        scratch_shapes=[pltpu.VMEM((len(heads), s, LANES), BF16),
                        pltpu.VMEM((len(heads), s, LANES), BF16),
                        pltpu.VMEM((len(heads), t, LANES), F32),
                        pltpu.SMEM((len(heads),), F32)],
        compiler_params=_cparams(("parallel", "arbitrary")),
        name="fox",
    )(qkv, qkv, c, c)


def _unit_lower_inverses(lbds, c):
    n = lbds[0].shape[0]
    row = _iota((n, n), 0)
    col = _iota((n, n), 1)
    eye = (row == col).astype(F32)
    same2 = _blk(row, 2) == _blk(col, 2)
    xs = [eye + jnp.where(same2, l, 0.0) for l in lbds]
    s = 2
    while s < c:
        join = (_blk(row, 2 * s) == _blk(col, 2 * s)) & (_blk(row, s) != _blk(col, s))
        ts = [_dot(jnp.where(join, l, 0.0), x) for l, x in zip(lbds, xs)]
        yield True
        xs = [x + _dot(x, t) for x, t in zip(xs, ts)]
        yield True
        s *= 2
    return xs


def _rwkv_body(pa_ref, mu_ref, wl_ref, w0_ref, a0_ref, kk_ref, ka_ref, rk_ref, lnw_ref, lnb_ref,
               o_ref, prev_ref, h_ref, *, tb, c, da):
    n_pairs = da // LANES
    n_chunks = tb // c
    width = pa_ref.shape[-1]
    assert 2 * c == LANES

    p = pa_ref[0]
    prow = _iota((tb, width), 0)
    prev = jnp.where(prow == 0, prev_ref[...], pltpu.roll(p, 1, 0))
    prev_ref[...] = p[tb - 1:tb, :]
    xs = p + mu_ref[...] * (prev - p)
    r = xs[:, 0:da]
    k = xs[:, da:2 * da]
    v = xs[:, 2 * da:3 * da]
    wa = xs[:, 3 * da:3 * da + LANES]
    lane_t = _iota((tb, LANES), 1)
    wa = jnp.where(lane_t < HEAD_DIM, jnp.tanh(wa), wa)
    lora = _dot(wa, wl_ref[...])
    wpre = w0_ref[...] + lora[:, 0:da]
    w = -(jnp.maximum(-wpre, 0.0) + jnp.log(1.0 + jnp.exp(-jnp.abs(wpre)))) - 0.5
    lw = -jnp.exp(w)
    a = _sigmoid(a0_ref[...] + lora[:, da:2 * da])
    k2 = k * (1.0 + (a - 1.0) * ka_ref[...])
    kk = k * kk_ref[...]

    er = _blk(_iota((LANES, LANES), 0), HEAD_DIM)
    ec = _blk(_iota((LANES, LANES), 1), HEAD_DIM)
    bd_mask = er == ec
    ones_head = bd_mask.astype(BF16)
    diag_eye = (_iota((LANES, LANES), 0) == _iota((LANES, LANES), 1)).astype(F32)

    pairs = [_lanes(i) for i in range(n_pairs)]
    kkn = jnp.concatenate(
        [kk[:, sl] * lax.rsqrt(jnp.maximum(_dot_xr(kk[:, sl] * kk[:, sl], ones_head, 1), 1e-24))
         for sl in pairs], axis=1)
    av = -kkn
    bv = kkn * a

    grp = min(tb, 2 * LANES)
    tr = _iota((grp, grp), 0)
    tc = _iota((grp, grp), 1)
    tri_blk = ((_blk(tr, c) == _blk(tc, c)) & (tc <= tr)).astype(BF16)
    cum = jnp.concatenate([_dot_lx(tri_blk, lw[i:i + grp], 2) for i in range(0, tb, grp)], axis=0)
    eg = jnp.exp(cum)
    rt = r * eg
    at = av * jnp.exp(cum - lw)
    ieg = jnp.exp(-cum)
    kt = k2 * ieg
    bt = bv * ieg

    rr = _iota((2 * c, 2 * c), 0)
    cc = _iota((2 * c, 2 * c), 1)
    same_blk = _blk(rr, c) == _blk(cc, c)
    strict_2c = _rem(cc, c) < _rem(rr, c)
    incl_2c = _rem(cc, c) <= _rem(rr, c)
    m0 = _iota((c, LANES), 1) < HEAD_DIM
    m0w = _rem(_iota((c, 2 * LANES), 1), LANES) < HEAD_DIM
    zeros_c = jnp.zeros((c, LANES), F32)

    items = [(sl, slice(j * c, (j + 1) * c)) for sl in pairs for j in range(n_chunks)]
    lbds, laks, prms, qa2s, vstk, gams, bkbars = [], [], [], [], [], [], []
    for sl, rs in items:
        rc, ac, kc, bc, vc = rt[rs, sl], at[rs, sl], kt[rs, sl], bt[rs, sl], v[rs, sl]
        gam = eg[rs.stop - 1:rs.stop, sl]
        bk = jnp.concatenate([bc, kc], axis=0)
        at0 = jnp.where(m0, ac, 0.0)
        at1 = jnp.where(m0, 0.0, ac)
        qa4 = jnp.concatenate([at0, at1, jnp.where(m0, rc, 0.0), jnp.where(m0, 0.0, rc)], axis=0)
        pm = _dot_nt(qa4, bk)
        xall = jnp.concatenate([pm[0:c], pltpu.roll(pm[c:2 * c], c, 1)], axis=0)
        lbds.append(jnp.where(same_blk & strict_2c, xall, 0.0))
        laks.append(jnp.where((~same_blk) & strict_2c, xall, 0.0))
        prms.append(jnp.where(incl_2c, pm[2 * c:], 0.0))
        qa2s.append(jnp.concatenate([at0, at1], axis=0))
        vstk.append(jnp.concatenate([jnp.where(m0, 0.0, vc), jnp.where(m0, vc, 0.0)], axis=0))
        gams.append(gam)
        bkbars.append(jnp.concatenate([bc * gam, kc * gam], axis=0))

    yield True
    tinvs = yield from _unit_lower_inverses(lbds, c)
    w1s = [_dot(lak, vs) for lak, vs in zip(laks, vstk)]
    tzs = [_dot(tinv, jnp.concatenate([qa2, w1], axis=1))
           for tinv, qa2, w1 in zip(tinvs, qa2s, w1s)]
    r3s, y2s, phis, gs = [], [], [], []
    for (sl, rs), tz, prm, bkbar, gam in zip(items, tzs, prms, bkbars, gams):
        a2v2 = tz[0:c] + tz[c:2 * c]
        rhs = jnp.concatenate([a2v2, jnp.concatenate([zeros_c, v[rs, sl]], axis=1)], axis=0)
        res = _dot(jnp.concatenate([prm, bkbar.T], axis=0), rhs)
        ry = jnp.where(m0w, res[0:c], res[c:2 * c])
        r3s.append(rt[rs, sl] + ry[:, 0:LANES])
        y2s.append(ry[:, LANES:])
        pg = res[2 * c:]
        phis.append(diag_eye * gam + jnp.where(bd_mask, pg[:, 0:LANES], 0.0))
        gs.append(jnp.where(bd_mask, pg[:, LANES:], 0.0))

    hs = [h_ref[i] for i in range(n_pairs)]
    ys = [[] for _ in range(n_pairs)]
    for j in range(n_chunks):
        for i in range(n_pairs):
            idx = i * n_chunks + j
            ys[i].append(_dot(r3s[idx], hs[i]) + y2s[idx])
        hs = [_dot(phis[i * n_chunks + j], hs[i]) + gs[i * n_chunks + j] for i in range(n_pairs)]
    for i in range(n_pairs):
        h_ref[i] = hs[i]
    bonus = [_dot_xr(r[:, sl] * k2[:, sl] * rk_ref[:, sl], ones_head, 1) * v[:, sl] for sl in pairs]
    yfull = [jnp.concatenate(ys[i], axis=0) for i in range(n_pairs)]
    ycs = [y - _dot_xr(y, ones_head, 2) * (1.0 / HEAD_DIM) for y in yfull]
    var = [_dot_xr(yc_ * yc_, ones_head, 1) * (1.0 / HEAD_DIM) for yc_ in ycs]
    for i, sl in enumerate(pairs):
        yn = ycs[i] * lax.rsqrt(var[i] + GN_EPS) * lnw_ref[:, sl] + lnb_ref[:, sl]
        o_ref[0, :, sl] = (yn + bonus[i]).astype(o_ref.dtype)


def _rwkv_sb_kernel(*refs, tb, c, da, t, heads):
    rwkv_in, (q_ref, kv_ref, ya_ref, yb_ref, prev_ref, h_ref) = refs[:10], refs[10:]
    ti = pl.program_id(1)

    @pl.when(ti == 0)
    def _():
        prev_ref[...] = jnp.zeros_like(prev_ref)
        h_ref[...] = jnp.zeros_like(h_ref)

    blocks = [_SbBlock(q_ref, kv_ref, yb_ref, ti * (tb // t) + j, slice(j * t, (j + 1) * t), heads)
              for j in range(tb // t)]
    sb_heads = itertools.chain(*[blk.front() for blk in blocks])
    for mxu_run_next in _rwkv_body(*rwkv_in, ya_ref, prev_ref, h_ref, tb=tb, c=c, da=da):
        if mxu_run_next:
            next(sb_heads, None)
    for _ in sb_heads:
        pass
    for blk in blocks:
        blk.finish()


def _rwkv_sb(pa, mu, wl, w0, a0, k_k, k_a, r_k, ln_w, ln_b, qkv, heads, n_out_blocks, sb_width,
             tb=512, c=64, t=256):
    b, s, width = pa.shape
    da = w0.shape[-1]
    vec = lambda n: pl.BlockSpec((1, n), lambda bi, ti: (0, 0))
    return pl.pallas_call(
        functools.partial(_rwkv_sb_kernel, tb=tb, c=c, da=da, t=t, heads=heads),
        grid=(b, s // tb),
        in_specs=[pl.BlockSpec((1, tb, width), lambda bi, ti: (bi, ti, 0)),
                  vec(width),
                  pl.BlockSpec(wl.shape, lambda bi, ti: (0, 0)),
                  vec(da), vec(da), vec(da), vec(da), vec(da), vec(da), vec(da),
                  pl.BlockSpec((1, tb, sb_width), lambda bi, ti: (bi, ti, 0)),
                  pl.BlockSpec((1, s, sb_width), lambda bi, ti: (bi, 0, 0))],
        out_specs=[pl.BlockSpec((1, tb, da), lambda bi, ti: (bi, ti, 0)),
                   pl.BlockSpec((1, tb, n_out_blocks * LANES), lambda bi, ti: (bi, ti, 0))],
        out_shape=[jax.ShapeDtypeStruct((b, s, da), BF16),
                   jax.ShapeDtypeStruct((b, s, n_out_blocks * LANES), BF16)],
        scratch_shapes=[pltpu.VMEM((1, width), F32),
                        pltpu.VMEM((da // LANES, LANES, LANES), F32)],
        compiler_params=_cparams(("parallel", "arbitrary")),
        name="rwkv7_stickbreak",
    )(pa, mu, wl, w0, a0, k_k, k_a, r_k, ln_w, ln_b, qkv, qkv)


def kernel(x, norm_w, w_in, b_f, mu, w0, w_up, a0, a_up, k_k, k_a, r_k, ln_x_w, ln_x_b, w_out,
           final_norm_w):
    b, s, d = x.shape
    depth = w_in.shape[0]
    da = w0.shape[-1]
    lora = w_up.shape[1]
    hc = b_f.shape[-1]
    d_mix = w_out.shape[1]
    db = (d_mix - da) // 2
    n_shift = 3 * da + 2 * lora
    n_qkv = 6 * db
    assert da % LANES == 0 and 2 * lora == LANES and db == hc * HEAD_DIM and hc % 2 == 1
    assert mu.shape[-1] == n_shift and (n_shift + n_qkv) % LANES == 0
    n_out_blocks = (db + HEAD_DIM) // LANES
    dbp = n_out_blocks * LANES

    n_main = n_shift + n_qkv
    q_ranges = ((n_shift, n_shift + db), (n_shift + 3 * db, n_shift + 4 * db))
    sb_heads = tuple(_Head(q=_slot(HEAD_DIM * g), k=_slot(db + HEAD_DIM * g),
                           v=_slot(2 * db + HEAD_DIM * g), out=_slot(HEAD_DIM * g))
                     for g in range(hc))
    fox_heads = tuple(
        _Head(q=_slot(3 * db + HEAD_DIM * j), k=_slot(4 * db + HEAD_DIM * j),
              v=_slot(5 * db + HEAD_DIM * j), out=_slot(HEAD_DIM * (j + 1)),
              c=HEAD_DIM * (1 - _slot(4 * db + HEAD_DIM * j)[1]) + BIAS_GROUP * j)
        for j in range(hc))
    f_bases = tuple(hd.c for hd in fox_heads)
    assert hc * BIAS_GROUP <= HEAD_DIM
    w_all = _wprep_in(w_in, n_main, q_ranges, f_bases)
    wo_all = _wprep_out(w_out)
    sb_width = -(-(3 * db) // (2 * LANES)) * 2 * LANES
    assert all(hd.out[1] == hd.v[1] for hd in sb_heads + fox_heads) and sb_width <= n_qkv

    x2 = x.reshape(b * s, d)
    zpad = lambda rows, n: jnp.zeros((rows, n), F32)
    for l in range(depth):
        widths = (n_shift, n_qkv, LANES, d_mix)
        dtypes = (F32, BF16, F32, BF16)
        pa, qkv, f, g = _inproj(x2, norm_w[l][None, :], w_all, l, widths, dtypes)

        wl_comb = jnp.concatenate(
            [jnp.concatenate([w_up[l], zpad(lora, da)], axis=1),
             jnp.concatenate([zpad(lora, da), a_up[l]], axis=1)], axis=0).astype(BF16)
        row = lambda t: t.reshape(1, -1)
        qkv3 = qkv.reshape(b, s, n_qkv)
        ya, yb = _rwkv_sb(pa.reshape(b, s, n_shift), row(mu[l]), wl_comb, row(w0[l]), row(a0[l]),
                          row(k_k[l]), row(k_a[l]), row(r_k[l]), row(ln_x_w[l]), row(ln_x_b[l]),
                          qkv3, sb_heads, n_out_blocks, sb_width)
        lane_ids = jnp.arange(LANES)
        bf_pad = jnp.zeros((LANES,), F32)
        for j, base in enumerate(f_bases):
            bf_pad = jnp.where((lane_ids >= base) & (lane_ids < base + 6), b_f[l, j], bf_pad)
        cum_f = _fcum(f.reshape(b, s, LANES), bf_pad[None, :])
        yc = _fox_attention(qkv3, cum_f, fox_heads, n_out_blocks)

        x2 = _outproj(x2, ya.reshape(b * s, da), yb.reshape(b * s, dbp), yc.reshape(b * s, dbp),
                      g, wo_all, l, final_norm_w[None, :], final=(l == depth - 1))
    return x2.reshape(b, s, d)
```
